```python
import jax, jax.numpy as jnp
from jax import lax
import numpy as np

D_MODEL = 2048
BATCH = 2
SEQ = 4096
DEPTH = 2
DEC_BATCH = 128
DEC_SEQ = 8
PAST_LEN = 8192
PAGE_SIZE = 128

N_A_LAYERS = DEPTH // 2
N_B_LAYERS = DEPTH - N_A_LAYERS
D_CONV = D_MODEL
CONV_WIDTH = 3
N_HEADS = 16
Q_LORA_RANK = 512
KV_LORA_RANK = 512
QK_NOPE_DIM = 128
QK_ROPE_DIM = 64
V_HEAD_DIM = 128
D_ATTN = N_HEADS * V_HEAD_DIM
SOFTMAX_SCALE = (QK_NOPE_DIM + QK_ROPE_DIM) ** -0.5
ROPE_THETA = 10000.0
RMS_EPS = 1e-6
Q_BLOCK = 128

kernel_name = "yoco_shortconv_mla_step"


def rms_norm(x, g):
    xf = x.astype(jnp.float32)
    y = xf * lax.rsqrt(jnp.mean(xf * xf, axis=-1, keepdims=True) + RMS_EPS)
    return (y * g.astype(jnp.float32)).astype(x.dtype)


def rope(x, pos):
    half = x.shape[-1] // 2
    inv_freq = ROPE_THETA ** (-jnp.arange(half, dtype=jnp.float32) / half)
    ang = pos.astype(jnp.float32)[:, None] * inv_freq[None, :]
    ang = ang.reshape((ang.shape[0],) + (1,) * (x.ndim - 3) + (half,))
    cos, sin = jnp.cos(ang).astype(x.dtype), jnp.sin(ang).astype(x.dtype)
    x1, x2 = x[..., :half], x[..., half:]
    return jnp.concatenate([x1 * cos - x2 * sin, x2 * cos + x1 * sin], axis=-1)


def modulated_prenorm(x, c, w_ada, b_ada, g_pre):
    mod = (jax.nn.silu(c) @ w_ada + b_ada)[:, None, :]
    shift, scale, gate = jnp.split(mod, 3, axis=-1)
    return rms_norm(x, g_pre) * (1 + scale) + shift, gate


def short_conv_mixer(h, conv_buf, w_in, w_conv, w_out):
    b_g, c_g, v, z = jnp.split(h @ w_in, 4, axis=-1)
    u = c_g * v
    padded = jnp.concatenate([conv_buf.astype(u.dtype), u], axis=1)
    t = h.shape[1]
    conv = w_conv[0] * padded[:, 0:t]
    for k in range(1, CONV_WIDTH):
        conv = conv + w_conv[k] * padded[:, k:k + t]
    y = b_g * conv * jax.nn.silu(z)
    return y @ w_out, padded[:, -(CONV_WIDTH - 1):]


def shared_latent_kv(x, pos, g_in, w_down, g_latent):
    ckv, kpe = jnp.split(rms_norm(x, g_in) @ w_down, [KV_LORA_RANK], axis=-1)
    return rms_norm(ckv, g_latent), rope(kpe, pos)


def mla_queries(h, pos, w_in, g_q, w_uq, w_kv_up):
    b, t = h.shape[:2]
    qa, z = jnp.split(h @ w_in, [Q_LORA_RANK], axis=-1)
    q = (rms_norm(qa, g_q) @ w_uq).reshape(b, t, N_HEADS, QK_NOPE_DIM + QK_ROPE_DIM)
    q_nope, q_pe = jnp.split(q, [QK_NOPE_DIM], axis=-1)
    q_pe = rope(q_pe, pos)
    q_lat = jnp.einsum('bthn,chn->bthc', q_nope, w_kv_up[..., :QK_NOPE_DIM])
    return q_lat, q_pe, z


def mla_output(o_lat, z, w_kv_up, w_out):
    b, t = o_lat.shape[:2]
    o = jnp.einsum('bthc,chv->bthv', o_lat, w_kv_up[..., QK_NOPE_DIM:]).reshape(b, t, D_ATTN)
    return (o * jax.nn.silu(z)) @ w_out


def prompt_latent_attention(q_lat, q_pe, ckv, kpe):
    b, s = q_lat.shape[:2]
    nb = s // Q_BLOCK
    ql = q_lat.reshape(b, nb, Q_BLOCK, N_HEADS, KV_LORA_RANK).transpose(1, 0, 2, 3, 4)
    qp = q_pe.reshape(b, nb, Q_BLOCK, N_HEADS, QK_ROPE_DIM).transpose(1, 0, 2, 3, 4)
    kpos = jnp.arange(s)

    def block(args):
        ql_b, qp_b, i = args
        sc = (jnp.einsum('bqhc,bkc->bhqk', ql_b, ckv)
              + jnp.einsum('bqhr,bkr->bhqk', qp_b, kpe)).astype(jnp.float32) * SOFTMAX_SCALE
        qpos = i * Q_BLOCK + jnp.arange(Q_BLOCK)
        sc = jnp.where(kpos[None, :] <= qpos[:, None], sc, -jnp.inf)
        p = jax.nn.softmax(sc, axis=-1).astype(ckv.dtype)
        return jnp.einsum('bhqk,bkc->bqhc', p, ckv)

    o = lax.map(block, (ql, qp, jnp.arange(nb)))
    return o.transpose(1, 0, 2, 3, 4).reshape(b, s, N_HEADS, KV_LORA_RANK)


def sample_latent_attention(q_lat, q_pe, ckv_new, kpe_new, cache_ckv, cache_kpe, page_table):
    bd, t = q_lat.shape[:2]
    past_ckv = cache_ckv[page_table].reshape(bd, -1, KV_LORA_RANK)
    past_kpe = cache_kpe[page_table].reshape(bd, -1, QK_ROPE_DIM)
    n_past = past_ckv.shape[1]
    s_past = (jnp.einsum('bqhc,bkc->bhqk', q_lat, past_ckv)
              + jnp.einsum('bqhr,bkr->bhqk', q_pe, past_kpe)).astype(jnp.float32) * SOFTMAX_SCALE
    s_new = (jnp.einsum('bqhc,bkc->bhqk', q_lat, ckv_new)
             + jnp.einsum('bqhr,bkr->bhqk', q_pe, kpe_new)).astype(jnp.float32) * SOFTMAX_SCALE
    causal = jnp.tril(jnp.ones((t, t), dtype=bool))
    s_new = jnp.where(causal, s_new, -jnp.inf)
    p = jax.nn.softmax(jnp.concatenate([s_past, s_new], axis=-1), axis=-1).astype(ckv_new.dtype)
    return (jnp.einsum('bhqk,bkc->bqhc', p[..., :n_past], past_ckv)
            + jnp.einsum('bhqk,bkc->bqhc', p[..., n_past:], ckv_new))


def setup_inputs(seed: int = 0) -> dict:
    key = jax.random.key(seed)
    ks = jax.random.split(key, 26)
    f32 = jnp.float32
    n_pages = PAST_LEN // PAGE_SIZE
    n_used = DEC_BATCH * n_pages
    n_pool = n_used + max(1, n_used // 4)
    nrm = lambda k, shape, s: jax.random.normal(k, shape, f32) * s
    page_table = jax.random.permutation(ks[0], n_pool)[:n_used].reshape(DEC_BATCH, n_pages).astype(jnp.int32)
    return {
        'x_prompt': nrm(ks[1], (BATCH, SEQ, D_MODEL), 1.0),
        'x_sample': nrm(ks[2], (DEC_BATCH, DEC_SEQ, D_MODEL), 1.0),
        'state_conv': nrm(ks[3], (N_A_LAYERS, DEC_BATCH, CONV_WIDTH - 1, D_CONV), 1.0),
        'cache_ckv': nrm(ks[4], (n_pool, PAGE_SIZE, KV_LORA_RANK), 1.0),
        'cache_kpe': nrm(ks[5], (n_pool, PAGE_SIZE, QK_ROPE_DIM), 1.0),
        'page_table': page_table,
        'c_prompt': nrm(ks[6], (BATCH, D_MODEL), 1.0),
        'c_sample': nrm(ks[7], (DEC_BATCH, D_MODEL), 1.0),
        'w_ada': nrm(ks[8], (DEPTH, D_MODEL, 3 * D_MODEL), 0.5 * D_MODEL ** -0.5),
        'b_ada': nrm(ks[9], (DEPTH, 3 * D_MODEL), 0.02),
        'g_pre': 1.0 + nrm(ks[10], (DEPTH, D_MODEL), 0.02),
        'g_post': 1.0 + nrm(ks[11], (DEPTH, D_MODEL), 0.02),
        'conv_w_in': nrm(ks[12], (N_A_LAYERS, D_MODEL, 4 * D_CONV), D_MODEL ** -0.5),
        'conv_w': nrm(ks[13], (N_A_LAYERS, CONV_WIDTH, D_CONV), CONV_WIDTH ** -0.5),
        'conv_w_out': nrm(ks[14], (N_A_LAYERS, D_CONV, D_MODEL), D_CONV ** -0.5),
        'kv_g_in': 1.0 + nrm(ks[15], (D_MODEL,), 0.02),
        'kv_w_down': nrm(ks[16], (D_MODEL, KV_LORA_RANK + QK_ROPE_DIM), D_MODEL ** -0.5),
        'kv_g_latent': 1.0 + nrm(ks[17], (KV_LORA_RANK,), 0.02),
        'kv_w_up': nrm(ks[18], (KV_LORA_RANK, N_HEADS, QK_NOPE_DIM + V_HEAD_DIM), KV_LORA_RANK ** -0.5),
        'mla_w_in': nrm(ks[19], (N_B_LAYERS, D_MODEL, Q_LORA_RANK + D_ATTN), D_MODEL ** -0.5),
        'mla_g_q': 1.0 + nrm(ks[20], (N_B_LAYERS, Q_LORA_RANK), 0.02),
        'mla_w_uq': nrm(ks[21], (N_B_LAYERS, Q_LORA_RANK, N_HEADS * (QK_NOPE_DIM + QK_ROPE_DIM)), Q_LORA_RANK ** -0.5),
        'mla_w_out': nrm(ks[22], (N_B_LAYERS, D_ATTN, D_MODEL), D_ATTN ** -0.5),
    }


def reference(x_prompt, x_sample, state_conv, cache_ckv, cache_kpe, page_table, c_prompt, c_sample,
              w_ada, b_ada, g_pre, g_post, conv_w_in, conv_w, conv_w_out,
              kv_g_in, kv_w_down, kv_g_latent, kv_w_up,
              mla_w_in, mla_g_q, mla_w_uq, mla_w_out):
    xp, xs = x_prompt, x_sample
    pos_p = jnp.arange(xp.shape[1])
    pos_s = PAST_LEN + jnp.arange(xs.shape[1])
    conv_p, conv_s = [], []
    ckv_p = kpe_p = ckv_s = kpe_s = None
    for l in range(DEPTH):
        if l == N_A_LAYERS:
            ckv_p, kpe_p = shared_latent_kv(xp, pos_p, kv_g_in, kv_w_down, kv_g_latent)
            ckv_s, kpe_s = shared_latent_kv(xs, pos_s, kv_g_in, kv_w_down, kv_g_latent)
        hp, gate_p = modulated_prenorm(xp, c_prompt, w_ada[l], b_ada[l], g_pre[l])
        hs, gate_s = modulated_prenorm(xs, c_sample, w_ada[l], b_ada[l], g_pre[l])
        if l < N_A_LAYERS:
            a = l
            zero_buf = jnp.zeros((xp.shape[0], CONV_WIDTH - 1, D_CONV), hp.dtype)
            op, buf_p = short_conv_mixer(hp, zero_buf, conv_w_in[a], conv_w[a], conv_w_out[a])
            os_, buf_s = short_conv_mixer(hs, state_conv[a], conv_w_in[a], conv_w[a], conv_w_out[a])
            conv_p.append(buf_p)
            conv_s.append(buf_s)
        else:
            b = l - N_A_LAYERS
            ql, qpe, z = mla_queries(hp, pos_p, mla_w_in[b], mla_g_q[b], mla_w_uq[b], kv_w_up)
            op = mla_output(prompt_latent_attention(ql, qpe, ckv_p, kpe_p), z, kv_w_up, mla_w_out[b])
            ql, qpe, z = mla_queries(hs, pos_s, mla_w_in[b], mla_g_q[b], mla_w_uq[b], kv_w_up)
            o_lat = sample_latent_attention(ql, qpe, ckv_s, kpe_s, cache_ckv, cache_kpe, page_table)
            os_ = mla_output(o_lat, z, kv_w_up, mla_w_out[b])
        xp = xp + gate_p * rms_norm(op, g_post[l])
        xs = xs + gate_s * rms_norm(os_, g_post[l])
    conv_state_p = jnp.stack(conv_p, axis=0)
    conv_state_s = jnp.stack(conv_s, axis=0)
    return (xp, xs, conv_state_p, conv_state_s, ckv_p, kpe_p, ckv_s, kpe_s)
```

```python
import functools

import jax
import jax.numpy as jnp
from jax import lax
from jax.experimental import pallas as pl
from jax.experimental.pallas import tpu as pltpu

F32 = jnp.float32
BF16 = jnp.bfloat16
RMS_EPS = 1e-6
ROPE_THETA = 10000.0
MIB = 1024 * 1024
NEG_INF = float("-inf")

N_HEADS = 16
QK_NOPE_DIM = 128
QK_ROPE_DIM = 64
V_HEAD_DIM = 128
SOFTMAX_SCALE = (QK_NOPE_DIM + QK_ROPE_DIM) ** -0.5


def _dot(a, b):
    return jnp.dot(a, b, preferred_element_type=F32)


def _dot_nt(a, b):
    return lax.dot_general(a, b, (((1,), (1,)), ((), ())), preferred_element_type=F32)


def _silu(x):
    return x * jax.nn.sigmoid(x)


def _unit_rms(x):
    return x * lax.rsqrt(jnp.mean(x * x, axis=-1, keepdims=True) + RMS_EPS)


def _params(semantics, vmem_mib):
    return pltpu.CompilerParams(dimension_semantics=semantics, vmem_limit_bytes=vmem_mib * MIB)


def _resident(block_shape, index_map):
    return pl.BlockSpec(block_shape, index_map, pipeline_mode=pl.Buffered(1))


def _ada_body(c_ref, w_ref, b_ref, o_ref):
    a = _silu(c_ref[...]).astype(BF16)
    o_ref[0] = _dot(a, w_ref[0].astype(BF16)) + b_ref[0]


def _ada(c_all, w_ada, b_ada):
    depth, d, d3 = w_ada.shape
    m = c_all.shape[0]
    tn = 1024
    return pl.pallas_call(
        _ada_body,
        grid=(depth, d3 // tn),
        in_specs=[
            pl.BlockSpec((m, d), lambda l, j: (0, 0)),
            pl.BlockSpec((1, d, tn), lambda l, j: (l, 0, j)),
            pl.BlockSpec((1, 1, tn), lambda l, j: (l, 0, j)),
        ],
        out_specs=pl.BlockSpec((1, m, tn), lambda l, j: (l, 0, j)),
        out_shape=jax.ShapeDtypeStruct((depth, m, d3), F32),
        compiler_params=_params(("arbitrary", "arbitrary"), 40),
        name="ada",
    )(c_all, w_ada, b_ada.reshape(depth, 1, d3))


def _conv_in_body(*refs, sample, tiles_per_seq):
    if sample:
        (x_ref, shift_ref, scale_ref, gpre_ref, st_ref, wb_ref, wc_ref, wv_ref, wz_ref, wconv_ref,
         y_ref, cs_ref, h_scr) = refs
    else:
        (x_ref, shift_ref, scale_ref, gpre_ref, wb_ref, wc_ref, wv_ref, wz_ref, wconv_ref,
         y_ref, cs_ref, h_scr, carry_scr) = refs
    i = pl.program_id(0)
    j = pl.program_id(1)
    g, r, d = x_ref.shape
    rows = g * r

    @pl.when(j == 0)
    def _():
        xn = _unit_rms(x_ref[...])
        h = xn * gpre_ref[...][None] * (1.0 + scale_ref[...]) + shift_ref[...]
        h_scr[...] = h.reshape(rows, d).astype(BF16)

    h = h_scr[...]
    bg = _dot(h, wb_ref[...])
    cg = _dot(h, wc_ref[...])
    v = _dot(h, wv_ref[...])
    z = _dot(h, wz_ref[...])
    u = cg * v
    tn = u.shape[1]
    t = lax.broadcasted_iota(jnp.int32, (rows, tn), 0)
    if sample:
        t = t & (r - 1)
        st = st_ref[...]
        s0 = jnp.broadcast_to(st[:, 0:1, :], (g, r, tn)).reshape(rows, tn)
        s1 = jnp.broadcast_to(st[:, 1:2, :], (g, r, tn)).reshape(rows, tn)
    else:
        @pl.when(i % tiles_per_seq == 0)
        def _():
            carry_scr[j] = jnp.zeros(carry_scr.shape[1:], F32)

        carry = carry_scr[j]
        s0 = carry[6:7, :]
        s1 = carry[7:8, :]
    p1 = jnp.where(t == 0, s1, pltpu.roll(u, 1, 0))
    p2 = jnp.where(t == 0, s0, jnp.where(t == 1, s1, pltpu.roll(u, 2, 0)))
    w = wconv_ref[...]
    conv = w[0:1, :] * p2 + w[1:2, :] * p1 + w[2:3, :] * u
    y_ref[...] = (bg * conv * _silu(z)).astype(BF16)
    if sample:
        cs_ref[...] = u.reshape(g, r, tn)[:, r - 2:r, :]
    else:
        tail = u[rows - 8:rows, :]
        carry_scr[j] = tail
        cs_ref[0] = tail


def _conv_in(x, mod, g_pre, state, w_in, w_conv, *, sample):
    nseq, t, d = x.shape
    dc = w_conv.shape[1]
    tn = 512
    nj = dc // tn
    tm = 512
    if sample:
        g, r = tm // t, t
        n_i = nseq // g
        tps = 1
        x_spec = pl.BlockSpec((g, r, d), lambda i, j: (i, 0, 0))
        mod_spec = lambda k: pl.BlockSpec((g, 1, d), lambda i, j: (i, 0, k))
    else:
        g, r = 1, tm
        tps = t // tm
        n_i = nseq * tps
        x_spec = pl.BlockSpec((1, tm, d), lambda i, j: (i // tps, i % tps, 0))
        mod_spec = lambda k: pl.BlockSpec((1, 1, d), lambda i, j: (i // tps, 0, k))
    w_spec = lambda s: pl.BlockSpec((d, tn), lambda i, j: (0, s * nj + j))
    in_specs = [x_spec, mod_spec(0), mod_spec(1), pl.BlockSpec((1, d), lambda i, j: (0, 0))]
    args = [x, mod, mod, g_pre.reshape(1, d)]
    scratch = [pltpu.VMEM((g * r, d), BF16)]
    if sample:
        in_specs.append(pl.BlockSpec((g, 2, tn), lambda i, j: (i, 0, j)))
        args.append(state)
        cs_shape = jax.ShapeDtypeStruct((nseq, 2, dc), F32)
        cs_spec = pl.BlockSpec((g, 2, tn), lambda i, j: (i, 0, j))
    else:
        cs_shape = jax.ShapeDtypeStruct((n_i, 8, dc), F32)
        cs_spec = pl.BlockSpec((1, 8, tn), lambda i, j: (i, 0, j))
        scratch.append(pltpu.VMEM((nj, 8, tn), F32))
    in_specs += [w_spec(0), w_spec(1), w_spec(2), w_spec(3), pl.BlockSpec((3, tn), lambda i, j: (0, j))]
    args += [w_in, w_in, w_in, w_in, w_conv]
    y, cs = pl.pallas_call(
        functools.partial(_conv_in_body, sample=sample, tiles_per_seq=tps),
        grid=(n_i, nj),
        in_specs=in_specs,
        out_specs=[pl.BlockSpec((g * r, tn), lambda i, j: (i, j)), cs_spec],
        out_shape=[jax.ShapeDtypeStruct((nseq * t, dc), BF16), cs_shape],
        scratch_shapes=scratch,
        compiler_params=_params(("arbitrary", "arbitrary"), 56),
        name="conv_in_sample" if sample else "conv_in_prompt",
    )(*args)
    if not sample:
        cs = cs.reshape(nseq, tps, 8, dc)[:, tps - 1, 6:8, :]
    return y, cs


def _out_proj_body(y_ref, x_ref, gate_ref, gpost_ref, w_ref, o_ref):
    g, r, d = x_ref.shape
    o = _dot(y_ref[...].astype(BF16), w_ref[...])
    o = _unit_rms(o) * gpost_ref[...]
    o_ref[...] = x_ref[...] + gate_ref[...] * o.reshape(g, r, d)


def _out_proj(y, x, mod, g_post, w_out, *, name):
    nseq, t, d = x.shape
    tm = 512
    if t >= tm:
        g, r = 1, tm
        tps = t // tm
        x_spec = pl.BlockSpec((1, tm, d), lambda i: (i // tps, i % tps, 0))
        gate_spec = pl.BlockSpec((1, 1, d), lambda i: (i // tps, 0, 2))
    else:
        g, r = tm // t, t
        tps = 1
        x_spec = pl.BlockSpec((g, r, d), lambda i: (i, 0, 0))
        gate_spec = pl.BlockSpec((g, 1, d), lambda i: (i, 0, 2))
    n_i = nseq * t // tm
    return pl.pallas_call(
        _out_proj_body,
        grid=(n_i,),
        in_specs=[
            pl.BlockSpec((tm, y.shape[1]), lambda i: (i, 0)),
            x_spec,
            gate_spec,
            pl.BlockSpec((1, d), lambda i: (0, 0)),
            _resident(w_out.shape, lambda i: (0, 0)),
        ],
        out_specs=x_spec,
        out_shape=jax.ShapeDtypeStruct(x.shape, F32),
        compiler_params=_params(("arbitrary",), 48),
        name=name,
    )(y, x, mod, g_post.reshape(1, d), w_out)


def _mla_in_body(x_ref, shift_ref, scale_ref, gpre_ref, gkv_ref, wdown_ref, glat_ref, cos_ref, sin_ref,
                 win_ref, gq_ref, wuq_ref, wuk_ref,
                 ckv_ref, kpe_ref, ckvb_ref, kpeb_ref, qlat_ref, qpe_ref, z_ref, *, sample):
    g, r, d = x_ref.shape
    rows = g * r
    lat = glat_ref.shape[1]
    rope = QK_ROPE_DIM
    xn = _unit_rms(x_ref[...])
    cos = cos_ref[...]
    sin = sin_ref[...]

    a = (xn * gkv_ref[...][None]).reshape(rows, d).astype(BF16)
    raw = _dot(a, wdown_ref[...])
    ckv = _unit_rms(raw[:, :lat]) * glat_ref[...]
    kpe = raw[:, lat:lat + rope] * cos[:, :rope] + raw[:, lat + 128:lat + 128 + rope] * sin[:, :rope]
    ckv_ref[...] = ckv
    kpe_ref[...] = kpe
    ckvb_ref[...] = ckv.astype(BF16)
    kpeb_ref[...] = kpe.astype(BF16)

    h = xn * gpre_ref[...][None] * (1.0 + scale_ref[...]) + shift_ref[...]
    qz = _dot(h.reshape(rows, d).astype(BF16), win_ref[...])
    qr = gq_ref.shape[1]
    z_ref[...] = qz[:, qr:]
    qa = (_unit_rms(qz[:, :qr]) * gq_ref[...]).astype(BF16)
    q = _dot(qa, wuq_ref[...]) * SOFTMAX_SCALE
    n_nope = N_HEADS * QK_NOPE_DIM
    n_rope = N_HEADS * rope
    for hd in range(N_HEADS):
        qn = q[:, hd * QK_NOPE_DIM:(hd + 1) * QK_NOPE_DIM].astype(BF16)
        ql = _dot(qn, wuk_ref[hd])
        if sample:
            qlat_ref[:, hd * r:(hd + 1) * r, :] = ql.reshape(g, r, lat)
        else:
            qlat_ref[hd] = ql.astype(BF16)
    for pr in range(N_HEADS // 2):
        lo = n_nope + pr * 128
        pe = q[:, lo:lo + 128] * cos + q[:, lo + n_rope:lo + n_rope + 128] * sin
        for k in range(2):
            hd = 2 * pr + k
            pe_h = pe[:, k * rope:(k + 1) * rope]
            if sample:
                qpe_ref[:, hd * r:(hd + 1) * r, :] = pe_h.reshape(g, r, rope)
            else:
                qpe_ref[hd] = pe_h.astype(BF16)


def _mla_in(x, mod, g_pre, kv_g_in, w_down_ext, kv_g_latent, cos_tab, sin_tab, w_in, g_q, w_uq_ext, w_uk_t,
            *, sample):
    nseq, t, d = x.shape
    n_tok = nseq * t
    lat = kv_g_latent.shape[0]
    rope = QK_ROPE_DIM
    d_attn = w_in.shape[1] - g_q.shape[0]
    tm = 256
    n_i = n_tok // tm
    row_spec = lambda w: pl.BlockSpec((tm, w), lambda i: (i, 0))
    if sample:
        g, r = tm // t, t
        x_spec = pl.BlockSpec((g, r, d), lambda i: (i, 0, 0))
        mod_spec = lambda k: pl.BlockSpec((g, 1, d), lambda i: (i, 0, k))
        tab_spec = pl.BlockSpec((tm, 128), lambda i: (0, 0))
        q_shapes = [jax.ShapeDtypeStruct((nseq, N_HEADS * t, lat), F32),
                    jax.ShapeDtypeStruct((nseq, N_HEADS * t, rope), F32)]
        q_specs = [pl.BlockSpec((g, N_HEADS * t, lat), lambda i: (i, 0, 0)),
                   pl.BlockSpec((g, N_HEADS * t, rope), lambda i: (i, 0, 0))]
    else:
        tps = t // tm
        x_spec = pl.BlockSpec((1, tm, d), lambda i: (i // tps, i % tps, 0))
        mod_spec = lambda k: pl.BlockSpec((1, 1, d), lambda i: (i // tps, 0, k))
        tab_spec = pl.BlockSpec((tm, 128), lambda i: (i % tps, 0))
        q_shapes = [jax.ShapeDtypeStruct((N_HEADS, n_tok, lat), BF16),
                    jax.ShapeDtypeStruct((N_HEADS, n_tok, rope), BF16)]
        q_specs = [pl.BlockSpec((N_HEADS, tm, lat), lambda i: (0, i, 0)),
                   pl.BlockSpec((N_HEADS, tm, rope), lambda i: (0, i, 0))]
    vec = lambda n: pl.BlockSpec((1, n), lambda i: (0, 0))
    return pl.pallas_call(
        functools.partial(_mla_in_body, sample=sample),
        grid=(n_i,),
        in_specs=[
            x_spec, mod_spec(0), mod_spec(1), vec(d), vec(d),
            _resident(w_down_ext.shape, lambda i: (0, 0)),
            vec(lat), tab_spec, tab_spec,
            _resident(w_in.shape, lambda i: (0, 0)),
            vec(g_q.shape[0]),
            _resident(w_uq_ext.shape, lambda i: (0, 0)),
            _resident(w_uk_t.shape, lambda i: (0, 0, 0)),
        ],
        out_specs=[row_spec(lat), row_spec(rope), row_spec(lat), row_spec(rope)] + q_specs + [row_spec(d_attn)],
        out_shape=[
            jax.ShapeDtypeStruct((n_tok, lat), F32),
            jax.ShapeDtypeStruct((n_tok, rope), F32),
            jax.ShapeDtypeStruct((n_tok, lat), BF16),
            jax.ShapeDtypeStruct((n_tok, rope), BF16),
        ] + q_shapes + [jax.ShapeDtypeStruct((n_tok, d_attn), F32)],
        compiler_params=_params(("arbitrary",), 58),
        name="mla_in_sample" if sample else "mla_in_prompt",
    )(x, mod, mod, g_pre.reshape(1, d), kv_g_in.reshape(1, d), w_down_ext, kv_g_latent.reshape(1, lat),
      cos_tab, sin_tab, w_in, g_q.reshape(1, -1), w_uq_ext, w_uk_t)


def _softmax_step(s, k, m_scr, l_scr, acc_scr):
    m_prev = m_scr[...]
    m_new = jnp.maximum(m_prev, jnp.max(s, axis=1, keepdims=True))
    alpha = jnp.exp(m_prev - m_new)
    p = jnp.exp(s - m_new)
    l_scr[...] = alpha * l_scr[...] + jnp.sum(p, axis=1, keepdims=True)
    acc_scr[...] = alpha * acc_scr[...] + _dot(p.astype(BF16), k)
    m_scr[...] = m_new


def _softmax_init(m_scr, l_scr, acc_scr):
    m_scr[...] = jnp.full(m_scr.shape, NEG_INF, F32)
    l_scr[...] = jnp.zeros(l_scr.shape, F32)
    acc_scr[...] = jnp.zeros(acc_scr.shape, F32)


def _attn_prompt_body(qlat_ref, qpe_ref, ckv_ref, kpe_ref, z_ref, wuv_ref, y_ref, m_scr, l_scr, acc_scr,
                      *, tk, tiles_per_seq):
    nh, tq, lat = qlat_ref.shape
    rows = nh * tq
    qi = pl.program_id(0) % tiles_per_seq
    q = qlat_ref[...].reshape(rows, lat)
    qp = qpe_ref[...].reshape(rows, qpe_ref.shape[2])
    _softmax_init(m_scr, l_scr, acc_scr)

    def scores(kj):
        k = ckv_ref[pl.ds(pl.multiple_of(kj * tk, tk), tk), :]
        kp = kpe_ref[pl.ds(pl.multiple_of(kj * tk, tk), tk), :]
        return _dot_nt(q, k) + _dot_nt(qp, kp), k

    def full_block(kj, carry):
        s, k = scores(kj)
        _softmax_step(s, k, m_scr, l_scr, acc_scr)
        return carry

    n_full = (qi * tq) // tk
    lax.fori_loop(0, n_full, full_block, 0)

    s, k = scores(n_full)
    q_pos = qi * tq + (lax.broadcasted_iota(jnp.int32, (rows, tk), 0) & (tq - 1))
    k_pos = n_full * tk + lax.broadcasted_iota(jnp.int32, (rows, tk), 1)
    _softmax_step(jnp.where(k_pos <= q_pos, s, NEG_INF), k, m_scr, l_scr, acc_scr)

    inv_l = 1.0 / l_scr[...]
    vd = wuv_ref.shape[2]
    for hd in range(nh):
        ol = (acc_scr[hd * tq:(hd + 1) * tq, :] * inv_l[hd * tq:(hd + 1) * tq, :]).astype(BF16)
        o = _dot(ol, wuv_ref[hd])
        y_ref[:, hd * vd:(hd + 1) * vd] = (o * _silu(z_ref[:, hd * vd:(hd + 1) * vd])).astype(BF16)


def _attn_prompt(qlat, qpe, ckv_b, kpe_b, z, w_uv, *, nseq):
    nh, n_tok, lat = qlat.shape
    rope = qpe.shape[2]
    t = n_tok // nseq
    tq, tk = 128, 256
    tps = t // tq
    d_attn = z.shape[1]
    return pl.pallas_call(
        functools.partial(_attn_prompt_body, tk=tk, tiles_per_seq=tps),
        grid=(n_tok // tq,),
        in_specs=[
            pl.BlockSpec((nh, tq, lat), lambda i: (0, i, 0)),
            pl.BlockSpec((nh, tq, rope), lambda i: (0, i, 0)),
            pl.BlockSpec((t, lat), lambda i: (i // tps, 0)),
            pl.BlockSpec((t, rope), lambda i: (i // tps, 0)),
            pl.BlockSpec((tq, d_attn), lambda i: (i, 0)),
            _resident(w_uv.shape, lambda i: (0, 0, 0)),
        ],
        out_specs=pl.BlockSpec((tq, d_attn), lambda i: (i, 0)),
        out_shape=jax.ShapeDtypeStruct((n_tok, d_attn), BF16),
        scratch_shapes=[pltpu.VMEM((nh * tq, 1), F32), pltpu.VMEM((nh * tq, 1), F32),
                        pltpu.VMEM((nh * tq, lat), F32)],
        compiler_params=_params(("arbitrary",), 56),
        name="attn_prompt",
    )(qlat, qpe, ckv_b, kpe_b, z, w_uv)


def _attn_sample_body(pt_ref, q_ref, qp_ref, *refs, n_pg):
    ck_pages = refs[:n_pg]
    kp_pages = refs[n_pg:2 * n_pg]
    (ckvn_ref, kpen_ref, z_ref, wuv_ref, y_ref, m_scr, l_scr, acc_scr, k_scr, kp_scr) = refs[2 * n_pg:]
    del pt_ref
    c = pl.program_id(1)
    page = ck_pages[0].shape[1]
    rows, lat = acc_scr.shape
    t_new = ckvn_ref.shape[0]

    @pl.when(c == 0)
    def _():
        _softmax_init(m_scr, l_scr, acc_scr)

    q = q_ref[0].astype(BF16)
    qp = qp_ref[0].astype(BF16)
    for k in range(n_pg):
        k_scr[k * page:(k + 1) * page, :] = ck_pages[k][0].astype(BF16)
        kp_scr[k * page:(k + 1) * page, :] = kp_pages[k][0].astype(BF16)
    kk = k_scr[...]
    _softmax_step(_dot_nt(q, kk) + _dot_nt(qp, kp_scr[...]), kk, m_scr, l_scr, acc_scr)

    @pl.when(c == pl.num_programs(1) - 1)
    def _():
        pad = page - t_new
        kn = jnp.concatenate([ckvn_ref[...], jnp.zeros((pad, lat), F32)], axis=0).astype(BF16)
        kpn = jnp.concatenate([kpen_ref[...], jnp.zeros((pad, kpen_ref.shape[1]), F32)], axis=0).astype(BF16)
        s = _dot_nt(q, kn) + _dot_nt(qp, kpn)
        t_q = lax.broadcasted_iota(jnp.int32, (rows, page), 0) & (t_new - 1)
        t_k = lax.broadcasted_iota(jnp.int32, (rows, page), 1)
        _softmax_step(jnp.where(t_k <= t_q, s, NEG_INF), kn, m_scr, l_scr, acc_scr)

        ol = (acc_scr[...] * (1.0 / l_scr[...])).astype(BF16)
        o_all = _dot(ol, wuv_ref[...])
        vd = wuv_ref.shape[1] // N_HEADS
        for hd in range(N_HEADS):
            o = o_all[hd * t_new:(hd + 1) * t_new, hd * vd:(hd + 1) * vd]
            y_ref[:, hd * vd:(hd + 1) * vd] = o * _silu(z_ref[:, hd * vd:(hd + 1) * vd])


def _attn_sample(qlat, qpe, cache_ckv, cache_kpe, page_table, ckv_new, kpe_new, z, w_uv_all):
    nseq, rows, lat = qlat.shape
    rope = qpe.shape[2]
    page = cache_ckv.shape[1]
    n_pages = page_table.shape[1]
    t_new = ckv_new.shape[0] // nseq
    d_attn = z.shape[1]
    n_pg = 16
    n_c = n_pages // n_pg

    def ck_spec(k):
        return pl.BlockSpec((1, page, lat), lambda b, c, pt: (pt[b, c * n_pg + k], 0, 0))

    def kp_spec(k):
        return pl.BlockSpec((1, page, rope), lambda b, c, pt: (pt[b, c * n_pg + k], 0, 0))

    grid_spec = pltpu.PrefetchScalarGridSpec(
        num_scalar_prefetch=1,
        grid=(nseq, n_c),
        in_specs=[
            pl.BlockSpec((1, rows, lat), lambda b, c, pt: (b, 0, 0)),
            pl.BlockSpec((1, rows, rope), lambda b, c, pt: (b, 0, 0)),
        ] + [ck_spec(k) for k in range(n_pg)] + [kp_spec(k) for k in range(n_pg)] + [
            pl.BlockSpec((t_new, lat), lambda b, c, pt: (b, 0)),
            pl.BlockSpec((t_new, rope), lambda b, c, pt: (b, 0)),
            pl.BlockSpec((t_new, d_attn), lambda b, c, pt: (b, 0)),
            pl.BlockSpec(w_uv_all.shape, lambda b, c, pt: (0, 0)),
        ],
        out_specs=pl.BlockSpec((t_new, d_attn), lambda b, c, pt: (b, 0)),
        scratch_shapes=[pltpu.VMEM((rows, 1), F32), pltpu.VMEM((rows, 1), F32), pltpu.VMEM((rows, lat), F32),
                        pltpu.VMEM((n_pg * page, lat), BF16), pltpu.VMEM((n_pg * page, rope), BF16)],
    )
    return pl.pallas_call(
        functools.partial(_attn_sample_body, n_pg=n_pg),
        grid_spec=grid_spec,
        out_shape=jax.ShapeDtypeStruct((nseq * t_new, d_attn), F32),
        compiler_params=_params(("arbitrary", "arbitrary"), 48),
        name="attn_sample",
    )(page_table, qlat, qpe, *([cache_ckv] * n_pg), *([cache_kpe] * n_pg), ckv_new, kpe_new, z, w_uv_all)


def _rope_tables(pos):
    half = QK_ROPE_DIM // 2
    inv_freq = ROPE_THETA ** (-jnp.arange(half, dtype=F32) / half)
    ang = pos.astype(F32)[:, None] * inv_freq[None, :]
    cos, sin = jnp.cos(ang), jnp.sin(ang)
    return jnp.concatenate([cos, cos, cos, cos], axis=-1), jnp.concatenate([-sin, sin, -sin, sin], axis=-1)


def _swap_halves(w, half):
    return jnp.concatenate([w[..., half:], w[..., :half]], axis=-1)


def kernel(x_prompt, x_sample, state_conv, cache_ckv, cache_kpe, page_table, c_prompt, c_sample, w_ada, b_ada, g_pre, g_post, conv_w_in, conv_w, conv_w_out, kv_g_in, kv_w_down, kv_g_latent, kv_w_up, mla_w_in, mla_g_q, mla_w_uq, mla_w_out):
    nb, seq, d = x_prompt.shape
    nd, t_dec, _ = x_sample.shape
    depth = w_ada.shape[0]
    n_a = conv_w_in.shape[0]
    assert depth == 2 and n_a == 1 and mla_w_in.shape[0] == 1
    lat = kv_g_latent.shape[0]
    rope = QK_ROPE_DIM
    half = rope // 2
    past_len = page_table.shape[1] * cache_ckv.shape[1]

    n_c = nb + nd
    n_c_pad = -(-n_c // 8) * 8
    c_all = jnp.concatenate([c_prompt, c_sample, jnp.zeros((n_c_pad - n_c, d), F32)], axis=0)
    mod = _ada(c_all, w_ada, b_ada)
    mod_p = [mod[l, :nb].reshape(nb, 1, 3 * d) for l in range(depth)]
    mod_s = [mod[l, nb:n_c].reshape(nd, 1, 3 * d) for l in range(depth)]

    w_in0 = conv_w_in[0].astype(BF16)
    w_out0 = conv_w_out[0].astype(BF16)
    y_p, conv_p = _conv_in(x_prompt, mod_p[0], g_pre[0], None, w_in0, conv_w[0], sample=False)
    xp = _out_proj(y_p, x_prompt, mod_p[0], g_post[0], w_out0, name="out_proj_conv_prompt")
    y_s, conv_s = _conv_in(x_sample, mod_s[0], g_pre[0], state_conv[0], w_in0, conv_w[0], sample=True)
    xs = _out_proj(y_s, x_sample, mod_s[0], g_post[0], w_out0, name="out_proj_conv_sample")

    zeros_pad = jnp.zeros((d, 128 - rope), F32)
    w_kpe = kv_w_down[:, lat:]
    w_down_ext = jnp.concatenate(
        [kv_w_down[:, :lat], w_kpe, zeros_pad, _swap_halves(w_kpe, half), zeros_pad], axis=1).astype(BF16)
    qr = mla_g_q.shape[1]
    w_uq = mla_w_uq[0].reshape(qr, N_HEADS, QK_NOPE_DIM + rope)
    w_uq_pe = w_uq[:, :, QK_NOPE_DIM:]
    w_uq_ext = jnp.concatenate(
        [w_uq[:, :, :QK_NOPE_DIM].reshape(qr, -1), w_uq_pe.reshape(qr, -1),
         _swap_halves(w_uq_pe, half).reshape(qr, -1)], axis=1).astype(BF16)
    w_uk_t = jnp.transpose(kv_w_up[:, :, :QK_NOPE_DIM], (1, 2, 0)).astype(BF16)
    w_uv = jnp.transpose(kv_w_up[:, :, QK_NOPE_DIM:], (1, 0, 2)).astype(BF16)
    w_uv_all = kv_w_up[:, :, QK_NOPE_DIM:].reshape(lat, -1).astype(BF16)
    w_in1 = mla_w_in[0].astype(BF16)
    w_out1 = mla_w_out[0].astype(BF16)

    cos_p, sin_p = _rope_tables(jnp.arange(seq))
    cos_s, sin_s = _rope_tables(past_len + jnp.arange(t_dec))
    reps = 256 // t_dec
    cos_s, sin_s = jnp.tile(cos_s, (reps, 1)), jnp.tile(sin_s, (reps, 1))

    ckv_p, kpe_p, ckvb_p, kpeb_p, qlat_p, qpe_p, z_p = _mla_in(
        xp, mod_p[1], g_pre[1], kv_g_in, w_down_ext, kv_g_latent, cos_p, sin_p, w_in1, mla_g_q[0], w_uq_ext,
        w_uk_t, sample=False)
    y_p = _attn_prompt(qlat_p, qpe_p, ckvb_p, kpeb_p, z_p, w_uv, nseq=nb)
    xp = _out_proj(y_p, xp, mod_p[1], g_post[1], w_out1, name="out_proj_mla_prompt")

    ckv_s, kpe_s, _, _, qlat_s, qpe_s, z_s = _mla_in(
        xs, mod_s[1], g_pre[1], kv_g_in, w_down_ext, kv_g_latent, cos_s, sin_s, w_in1, mla_g_q[0], w_uq_ext,
        w_uk_t, sample=True)
    y_s = _attn_sample(qlat_s, qpe_s, cache_ckv, cache_kpe, page_table, ckv_s, kpe_s, z_s, w_uv_all)
    xs = _out_proj(y_s, xs, mod_s[1], g_post[1], w_out1, name="out_proj_mla_sample")

    return (xp, xs, conv_p[None], conv_s[None],
            ckv_p.reshape(nb, seq, lat), kpe_p.reshape(nb, seq, rope),
            ckv_s.reshape(nd, t_dec, lat), kpe_s.reshape(nd, t_dec, rope))
```

```python
import functools

import jax
import jax.numpy as jnp
from jax import lax
from jax.experimental import pallas as pl
from jax.experimental.pallas import tpu as pltpu

F32 = jnp.float32
BF16 = jnp.bfloat16
RMS_EPS = 1e-6
ROPE_THETA = 10000.0
MIB = 1024 * 1024
NEG_INF = float("-inf")

N_HEADS = 16
QK_NOPE_DIM = 128
QK_ROPE_DIM = 64
V_HEAD_DIM = 128
SOFTMAX_SCALE = (QK_NOPE_DIM + QK_ROPE_DIM) ** -0.5


def _dot(a, b):
    return jnp.dot(a, b, preferred_element_type=F32)


def _dot_nt(a, b):
    return lax.dot_general(a, b, (((1,), (1,)), ((), ())), preferred_element_type=F32)


def _silu(x):
    return x * jax.nn.sigmoid(x)


def _unit_rms(x):
    return x * lax.rsqrt(jnp.mean(x * x, axis=-1, keepdims=True) + RMS_EPS)


def _params(semantics, vmem_mib):
    return pltpu.CompilerParams(dimension_semantics=semantics, vmem_limit_bytes=vmem_mib * MIB)


def _resident(block_shape, index_map):
    return pl.BlockSpec(block_shape, index_map, pipeline_mode=pl.Buffered(1))


def _ada_body(c_ref, w_ref, b_ref, o_ref):
    a = _silu(c_ref[...]).astype(BF16)
    o_ref[0] = _dot(a, w_ref[0].astype(BF16)) + b_ref[0]


def _ada(c_all, w_ada, b_ada):
    depth, d, d3 = w_ada.shape
    m = c_all.shape[0]
    tn = 1024
    return pl.pallas_call(
        _ada_body,
        grid=(depth, d3 // tn),
        in_specs=[
            pl.BlockSpec((m, d), lambda l, j: (0, 0)),
            pl.BlockSpec((1, d, tn), lambda l, j: (l, 0, j)),
            pl.BlockSpec((1, 1, tn), lambda l, j: (l, 0, j)),
        ],
        out_specs=pl.BlockSpec((1, m, tn), lambda l, j: (l, 0, j)),
        out_shape=jax.ShapeDtypeStruct((depth, m, d3), F32),
        compiler_params=_params(("arbitrary", "arbitrary"), 40),
        name="ada",
    )(c_all, w_ada, b_ada.reshape(depth, 1, d3))


def _conv_in_body(*refs, sample, tiles_per_seq):
    if sample:
        (x_ref, shift_ref, scale_ref, gpre_ref, st_ref, wb_ref, wc_ref, wv_ref, wz_ref, wconv_ref,
         y_ref, cs_ref, h_scr) = refs
    else:
        (x_ref, shift_ref, scale_ref, gpre_ref, wb_ref, wc_ref, wv_ref, wz_ref, wconv_ref,
         y_ref, cs_ref, h_scr, carry_scr) = refs
    i = pl.program_id(0)
    j = pl.program_id(1)
    g, r, d = x_ref.shape
    rows = g * r

    @pl.when(j == 0)
    def _():
        xn = _unit_rms(x_ref[...])
        h = xn * gpre_ref[...][None] * (1.0 + scale_ref[...]) + shift_ref[...]
        h_scr[...] = h.reshape(rows, d).astype(BF16)

    h = h_scr[...]
    bg = _dot(h, wb_ref[...])
    cg = _dot(h, wc_ref[...])
    v = _dot(h, wv_ref[...])
    z = _dot(h, wz_ref[...])
    u = cg * v
    tn = u.shape[1]
    t = lax.broadcasted_iota(jnp.int32, (rows, tn), 0)
    if sample:
        t = t & (r - 1)
        st = st_ref[...]
        s0 = jnp.broadcast_to(st[:, 0:1, :], (g, r, tn)).reshape(rows, tn)
        s1 = jnp.broadcast_to(st[:, 1:2, :], (g, r, tn)).reshape(rows, tn)
    else:
        @pl.when(i % tiles_per_seq == 0)
        def _():
            carry_scr[j] = jnp.zeros(carry_scr.shape[1:], F32)

        carry = carry_scr[j]
        s0 = carry[6:7, :]
        s1 = carry[7:8, :]
    p1 = jnp.where(t == 0, s1, pltpu.roll(u, 1, 0))
    p2 = jnp.where(t == 0, s0, jnp.where(t == 1, s1, pltpu.roll(u, 2, 0)))
    w = wconv_ref[...]
    conv = w[0:1, :] * p2 + w[1:2, :] * p1 + w[2:3, :] * u
    y_ref[...] = (bg * conv * _silu(z)).astype(BF16)
    if sample:
        cs_ref[...] = u.reshape(g, r, tn)[:, r - 2:r, :]
    else:
        tail = u[rows - 8:rows, :]
        carry_scr[j] = tail
        cs_ref[0] = tail


def _conv_in(x, mod, g_pre, state, w_in, w_conv, *, sample):
    nseq, t, d = x.shape
    dc = w_conv.shape[1]
    tn = 512
    nj = dc // tn
    tm = 512
    if sample:
        g, r = tm // t, t
        n_i = nseq // g
        tps = 1
        x_spec = pl.BlockSpec((g, r, d), lambda i, j: (i, 0, 0))
        mod_spec = lambda k: pl.BlockSpec((g, 1, d), lambda i, j: (i, 0, k))
    else:
        g, r = 1, tm
        tps = t // tm
        n_i = nseq * tps
        x_spec = pl.BlockSpec((1, tm, d), lambda i, j: (i // tps, i % tps, 0))
        mod_spec = lambda k: pl.BlockSpec((1, 1, d), lambda i, j: (i // tps, 0, k))
    w_spec = lambda s: pl.BlockSpec((d, tn), lambda i, j: (0, s * nj + j))
    in_specs = [x_spec, mod_spec(0), mod_spec(1), pl.BlockSpec((1, d), lambda i, j: (0, 0))]
    args = [x, mod, mod, g_pre.reshape(1, d)]
    scratch = [pltpu.VMEM((g * r, d), BF16)]
    if sample:
        in_specs.append(pl.BlockSpec((g, 2, tn), lambda i, j: (i, 0, j)))
        args.append(state)
        cs_shape = jax.ShapeDtypeStruct((nseq, 2, dc), F32)
        cs_spec = pl.BlockSpec((g, 2, tn), lambda i, j: (i, 0, j))
    else:
        cs_shape = jax.ShapeDtypeStruct((n_i, 8, dc), F32)
        cs_spec = pl.BlockSpec((1, 8, tn), lambda i, j: (i, 0, j))
        scratch.append(pltpu.VMEM((nj, 8, tn), F32))
    in_specs += [w_spec(0), w_spec(1), w_spec(2), w_spec(3), pl.BlockSpec((3, tn), lambda i, j: (0, j))]
    args += [w_in, w_in, w_in, w_in, w_conv]
    y, cs = pl.pallas_call(
        functools.partial(_conv_in_body, sample=sample, tiles_per_seq=tps),
        grid=(n_i, nj),
        in_specs=in_specs,
        out_specs=[pl.BlockSpec((g * r, tn), lambda i, j: (i, j)), cs_spec],
        out_shape=[jax.ShapeDtypeStruct((nseq * t, dc), BF16), cs_shape],
        scratch_shapes=scratch,
        compiler_params=_params(("arbitrary", "arbitrary"), 56),
        name="conv_in_sample" if sample else "conv_in_prompt",
    )(*args)
    if not sample:
        cs = cs.reshape(nseq, tps, 8, dc)[:, tps - 1, 6:8, :]
    return y, cs


def _out_proj_body(y_ref, x_ref, gate_ref, gpost_ref, w_ref, o_ref):
    g, r, d = x_ref.shape
    o = _dot(y_ref[...].astype(BF16), w_ref[...])
    o = _unit_rms(o) * gpost_ref[...]
    o_ref[...] = x_ref[...] + gate_ref[...] * o.reshape(g, r, d)


def _out_proj(y, x, mod, g_post, w_out, *, name):
    nseq, t, d = x.shape
    tm = 512
    if t >= tm:
        g, r = 1, tm
        tps = t // tm
        x_spec = pl.BlockSpec((1, tm, d), lambda i: (i // tps, i % tps, 0))
        gate_spec = pl.BlockSpec((1, 1, d), lambda i: (i // tps, 0, 2))
    else:
        g, r = tm // t, t
        tps = 1
        x_spec = pl.BlockSpec((g, r, d), lambda i: (i, 0, 0))
        gate_spec = pl.BlockSpec((g, 1, d), lambda i: (i, 0, 2))
    n_i = nseq * t // tm
    return pl.pallas_call(
        _out_proj_body,
        grid=(n_i,),
        in_specs=[
            pl.BlockSpec((tm, y.shape[1]), lambda i: (i, 0)),
            x_spec,
            gate_spec,
            pl.BlockSpec((1, d), lambda i: (0, 0)),
            _resident(w_out.shape, lambda i: (0, 0)),
        ],
        out_specs=x_spec,
        out_shape=jax.ShapeDtypeStruct(x.shape, F32),
        compiler_params=_params(("arbitrary",), 48),
        name=name,
    )(y, x, mod, g_post.reshape(1, d), w_out)


def _mla_in_body(x_ref, shift_ref, scale_ref, gpre_ref, gkv_ref, wdown_ref, glat_ref, cos_ref, sin_ref,
                 win_ref, gq_ref, wuq_ref, wuk_ref, *out_refs, sample):
    if sample:
        ckv_ref, kpe_ref, qlat_ref, qpe_ref, z_ref = out_refs
    else:
        ckv_ref, kpe_ref, kcat_ref, qcat_ref, z_ref = out_refs
    g, r, d = x_ref.shape
    rows = g * r
    lat = glat_ref.shape[1]
    rope = QK_ROPE_DIM
    xn = _unit_rms(x_ref[...])
    cos = cos_ref[...]
    sin = sin_ref[...]

    a = (xn * gkv_ref[...][None]).reshape(rows, d).astype(BF16)
    raw = _dot(a, wdown_ref[...])
    ckv = _unit_rms(raw[:, :lat]) * glat_ref[...]
    kpe2 = raw[:, lat:lat + 128] * cos + raw[:, lat + 128:lat + 256] * sin
    ckv_ref[...] = ckv
    kpe_ref[...] = kpe2[:, :rope]
    if not sample:
        kcat_ref[:, :lat] = ckv.astype(BF16)
        kcat_ref[:, lat:] = kpe2.astype(BF16)

    h = xn * gpre_ref[...][None] * (1.0 + scale_ref[...]) + shift_ref[...]
    qz = _dot(h.reshape(rows, d).astype(BF16), win_ref[...])
    qr = gq_ref.shape[1]
    z_ref[...] = qz[:, qr:]
    qa = (_unit_rms(qz[:, :qr]) * gq_ref[...]).astype(BF16)
    q = _dot(qa, wuq_ref[...]) * SOFTMAX_SCALE
    n_nope = N_HEADS * QK_NOPE_DIM
    n_rope = N_HEADS * rope
    for hd in range(N_HEADS):
        qn = q[:, hd * QK_NOPE_DIM:(hd + 1) * QK_NOPE_DIM].astype(BF16)
        ql = _dot(qn, wuk_ref[hd])
        if sample:
            qlat_ref[:, hd * r:(hd + 1) * r, :] = ql.reshape(g, r, lat)
        else:
            qcat_ref[hd, :, :lat] = ql.astype(BF16)
    lane = lax.broadcasted_iota(jnp.int32, (rows, 128), 1)
    for pr in range(N_HEADS // 2):
        lo = n_nope + pr * 128
        pe = q[:, lo:lo + 128] * cos + q[:, lo + n_rope:lo + n_rope + 128] * sin
        for k in range(2):
            hd = 2 * pr + k
            if sample:
                qpe_ref[:, hd * r:(hd + 1) * r, :] = pe[:, k * rope:(k + 1) * rope].reshape(g, r, rope)
            else:
                own = (lane < rope) if k == 0 else (lane >= rope)
                qcat_ref[hd, :, lat:] = jnp.where(own, pe, 0.0).astype(BF16)


def _mla_in(x, mod, g_pre, kv_g_in, w_down_ext, kv_g_latent, cos_tab, sin_tab, w_in, g_q, w_uq_ext, w_uk_t,
            *, sample):
    nseq, t, d = x.shape
    n_tok = nseq * t
    lat = kv_g_latent.shape[0]
    rope = QK_ROPE_DIM
    d_attn = w_in.shape[1] - g_q.shape[0]
    tm = 256
    n_i = n_tok // tm
    row_spec = lambda w: pl.BlockSpec((tm, w), lambda i: (i, 0))
    if sample:
        g, r = tm // t, t
        x_spec = pl.BlockSpec((g, r, d), lambda i: (i, 0, 0))
        mod_spec = lambda k: pl.BlockSpec((g, 1, d), lambda i: (i, 0, k))
        tab_spec = pl.BlockSpec((tm, 128), lambda i: (0, 0))
        qk_shapes = [jax.ShapeDtypeStruct((nseq, N_HEADS * t, lat), F32),
                     jax.ShapeDtypeStruct((nseq, N_HEADS * t, rope), F32)]
        qk_specs = [pl.BlockSpec((g, N_HEADS * t, lat), lambda i: (i, 0, 0)),
                    pl.BlockSpec((g, N_HEADS * t, rope), lambda i: (i, 0, 0))]
    else:
        tps = t // tm
        x_spec = pl.BlockSpec((1, tm, d), lambda i: (i // tps, i % tps, 0))
        mod_spec = lambda k: pl.BlockSpec((1, 1, d), lambda i: (i // tps, 0, k))
        tab_spec = pl.BlockSpec((tm, 128), lambda i: (i % tps, 0))
        qk_shapes = [jax.ShapeDtypeStruct((n_tok, lat + 128), BF16),
                     jax.ShapeDtypeStruct((N_HEADS, n_tok, lat + 128), BF16)]
        qk_specs = [row_spec(lat + 128), pl.BlockSpec((N_HEADS, tm, lat + 128), lambda i: (0, i, 0))]
    vec = lambda n: pl.BlockSpec((1, n), lambda i: (0, 0))
    return pl.pallas_call(
        functools.partial(_mla_in_body, sample=sample),
        grid=(n_i,),
        in_specs=[
            x_spec, mod_spec(0), mod_spec(1), vec(d), vec(d),
            _resident(w_down_ext.shape, lambda i: (0, 0)),
            vec(lat), tab_spec, tab_spec,
            _resident(w_in.shape, lambda i: (0, 0)),
            vec(g_q.shape[0]),
            _resident(w_uq_ext.shape, lambda i: (0, 0)),
            _resident(w_uk_t.shape, lambda i: (0, 0, 0)),
        ],
        out_specs=[row_spec(lat), row_spec(rope)] + qk_specs + [row_spec(d_attn)],
        out_shape=[jax.ShapeDtypeStruct((n_tok, lat), F32), jax.ShapeDtypeStruct((n_tok, rope), F32)]
        + qk_shapes + [jax.ShapeDtypeStruct((n_tok, d_attn), F32)],
        compiler_params=_params(("arbitrary",), 58),
        name="mla_in_sample" if sample else "mla_in_prompt",
    )(x, mod, mod, g_pre.reshape(1, d), kv_g_in.reshape(1, d), w_down_ext, kv_g_latent.reshape(1, lat),
      cos_tab, sin_tab, w_in, g_q.reshape(1, -1), w_uq_ext, w_uk_t)


LANES = 128


def _softmax_step(s, v, m_scr, l_scr, acc_scr):
    n_chunks = s.shape[1] // LANES
    chunks = [s[:, c * LANES:(c + 1) * LANES] for c in range(n_chunks)]
    m_cur = chunks[0]
    for ch in chunks[1:]:
        m_cur = jnp.maximum(m_cur, ch)
    m_prev = m_scr[...]
    m_new = jnp.maximum(m_prev, jnp.max(m_cur, axis=1, keepdims=True))
    alpha = jnp.exp(m_prev - m_new)
    ps = [jnp.exp(ch - m_new) for ch in chunks]
    p_sum = ps[0]
    for p in ps[1:]:
        p_sum = p_sum + p
    l_scr[...] = alpha * l_scr[...] + p_sum
    m_scr[...] = m_new
    pv = _dot(jnp.concatenate(ps, axis=1).astype(BF16), v)
    for c in range(acc_scr.shape[1] // LANES):
        sl = slice(c * LANES, (c + 1) * LANES)
        acc_scr[:, sl] = alpha * acc_scr[:, sl] + pv[:, sl]


def _softmax_init(m_scr, l_scr, acc_scr):
    m_scr[...] = jnp.full(m_scr.shape, NEG_INF, F32)
    l_scr[...] = jnp.zeros(l_scr.shape, F32)
    acc_scr[...] = jnp.zeros(acc_scr.shape, F32)


def _softmax_inv_denominator(l_scr):
    return 1.0 / jnp.sum(l_scr[...], axis=1, keepdims=True)


def _attn_prompt_body(q_ref, k_ref, z_ref, wuv_ref, y_ref, m_scr, l_scr, acc_scr, *, tk, tiles_per_seq):
    nh, tq, dk = q_ref.shape
    rows, lat = acc_scr.shape
    qi = pl.program_id(0) % tiles_per_seq
    q = q_ref[...].reshape(rows, dk)
    _softmax_init(m_scr, l_scr, acc_scr)

    def load_keys(kj):
        return k_ref[pl.ds(pl.multiple_of(kj * tk, tk), tk), :]

    def full_block(kj, carry):
        k = load_keys(kj)
        _softmax_step(_dot_nt(q, k), k[:, :lat], m_scr, l_scr, acc_scr)
        return carry

    n_full = (qi * tq) // tk
    lax.fori_loop(0, n_full, full_block, 0)

    k = load_keys(n_full)
    q_pos = qi * tq + (lax.broadcasted_iota(jnp.int32, (rows, tk), 0) & (tq - 1))
    k_pos = n_full * tk + lax.broadcasted_iota(jnp.int32, (rows, tk), 1)
    s = jnp.where(k_pos <= q_pos, _dot_nt(q, k), NEG_INF)
    _softmax_step(s, k[:, :lat], m_scr, l_scr, acc_scr)

    inv_l = _softmax_inv_denominator(l_scr)
    vd = wuv_ref.shape[2]
    for hd in range(nh):
        ol = (acc_scr[hd * tq:(hd + 1) * tq, :] * inv_l[hd * tq:(hd + 1) * tq, :]).astype(BF16)
        o = _dot(ol, wuv_ref[hd])
        y_ref[:, hd * vd:(hd + 1) * vd] = (o * _silu(z_ref[:, hd * vd:(hd + 1) * vd])).astype(BF16)


def _attn_prompt(qcat, kcat, z, w_uv, *, nseq):
    nh, n_tok, dk = qcat.shape
    lat = w_uv.shape[1]
    t = n_tok // nseq
    tq, tk = 128, 512
    tps = t // tq
    d_attn = z.shape[1]
    return pl.pallas_call(
        functools.partial(_attn_prompt_body, tk=tk, tiles_per_seq=tps),
        grid=(n_tok // tq,),
        in_specs=[
            pl.BlockSpec((nh, tq, dk), lambda i: (0, i, 0)),
            pl.BlockSpec((t, dk), lambda i: (i // tps, 0)),
            pl.BlockSpec((tq, d_attn), lambda i: (i, 0)),
            _resident(w_uv.shape, lambda i: (0, 0, 0)),
        ],
        out_specs=pl.BlockSpec((tq, d_attn), lambda i: (i, 0)),
        out_shape=jax.ShapeDtypeStruct((n_tok, d_attn), BF16),
        scratch_shapes=[pltpu.VMEM((nh * tq, LANES), F32), pltpu.VMEM((nh * tq, LANES), F32),
                        pltpu.VMEM((nh * tq, lat), F32)],
        compiler_params=_params(("arbitrary",), 56),
        name="attn_prompt",
    )(qcat, kcat, z, w_uv)


def _attn_sample_body(pt_ref, q_ref, qp_ref, *refs, n_pg):
    ck_pages = refs[:n_pg]
    kp_pages = refs[n_pg:2 * n_pg]
    (ckvn_ref, kpen_ref, z_ref, wuv_ref, y_ref, m_scr, l_scr, acc_scr, k_scr, kpt_scr) = refs[2 * n_pg:]
    del pt_ref
    c = pl.program_id(1)
    page = ck_pages[0].shape[1]
    rows, lat = acc_scr.shape
    t_new = ckvn_ref.shape[0]

    @pl.when(c == 0)
    def _():
        _softmax_init(m_scr, l_scr, acc_scr)

    q = q_ref[0].astype(BF16)
    qp = qp_ref[0].astype(BF16)
    for k in range(n_pg):
        k_scr[k * page:(k + 1) * page, :] = ck_pages[k][0].astype(BF16)
        kpt_scr[:, k * page:(k + 1) * page] = kp_pages[k][0].astype(BF16)
    kk = k_scr[...]
    _softmax_step(_dot_nt(q, kk) + _dot(qp, kpt_scr[...]), kk, m_scr, l_scr, acc_scr)

    @pl.when(c == pl.num_programs(1) - 1)
    def _():
        pad = page - t_new
        kn = jnp.concatenate([ckvn_ref[...], jnp.zeros((pad, lat), F32)], axis=0).astype(BF16)
        kpn = jnp.concatenate([kpen_ref[...], jnp.zeros((pad, kpen_ref.shape[1]), F32)], axis=0).astype(BF16)
        s = _dot_nt(q, kn) + _dot_nt(qp, kpn)
        t_q = lax.broadcasted_iota(jnp.int32, (rows, page), 0) & (t_new - 1)
        t_k = lax.broadcasted_iota(jnp.int32, (rows, page), 1)
        _softmax_step(jnp.where(t_k <= t_q, s, NEG_INF), kn, m_scr, l_scr, acc_scr)

        ol = (acc_scr[...] * _softmax_inv_denominator(l_scr)).astype(BF16)
        o_all = _dot(ol, wuv_ref[...])
        vd = wuv_ref.shape[1] // N_HEADS
        for hd in range(N_HEADS):
            o = o_all[hd * t_new:(hd + 1) * t_new, hd * vd:(hd + 1) * vd]
            y_ref[:, hd * vd:(hd + 1) * vd] = o * _silu(z_ref[:, hd * vd:(hd + 1) * vd])


def _attn_sample(qlat, qpe, cache_ckv, cache_kpe_t, page_table, ckv_new, kpe_new, z, w_uv_all):
    nseq, rows, lat = qlat.shape
    rope = qpe.shape[2]
    page = cache_ckv.shape[1]
    n_pages = page_table.shape[1]
    t_new = ckv_new.shape[0] // nseq
    d_attn = z.shape[1]
    n_pg = 16
    n_c = n_pages // n_pg

    def ck_spec(k):
        return pl.BlockSpec((1, page, lat), lambda b, c, pt: (pt[b, c * n_pg + k], 0, 0))

    def kp_spec(k):
        return pl.BlockSpec((1, rope, page), lambda b, c, pt: (pt[b, c * n_pg + k], 0, 0))

    grid_spec = pltpu.PrefetchScalarGridSpec(
        num_scalar_prefetch=1,
        grid=(nseq, n_c),
        in_specs=[
            pl.BlockSpec((1, rows, lat), lambda b, c, pt: (b, 0, 0)),
            pl.BlockSpec((1, rows, rope), lambda b, c, pt: (b, 0, 0)),
        ] + [ck_spec(k) for k in range(n_pg)] + [kp_spec(k) for k in range(n_pg)] + [
            pl.BlockSpec((t_new, lat), lambda b, c, pt: (b, 0)),
            pl.BlockSpec((t_new, rope), lambda b, c, pt: (b, 0)),
            pl.BlockSpec((t_new, d_attn), lambda b, c, pt: (b, 0)),
            pl.BlockSpec(w_uv_all.shape, lambda b, c, pt: (0, 0)),
        ],
        out_specs=pl.BlockSpec((t_new, d_attn), lambda b, c, pt: (b, 0)),
        scratch_shapes=[pltpu.VMEM((rows, LANES), F32), pltpu.VMEM((rows, LANES), F32),
                        pltpu.VMEM((rows, lat), F32),
                        pltpu.VMEM((n_pg * page, lat), BF16), pltpu.VMEM((rope, n_pg * page), BF16)],
    )
    return pl.pallas_call(
        functools.partial(_attn_sample_body, n_pg=n_pg),
        grid_spec=grid_spec,
        out_shape=jax.ShapeDtypeStruct((nseq * t_new, d_attn), F32),
        compiler_params=_params(("arbitrary", "arbitrary"), 48),
        name="attn_sample",
    )(page_table, qlat, qpe, *([cache_ckv] * n_pg), *([cache_kpe_t] * n_pg), ckv_new, kpe_new, z, w_uv_all)


def _rope_tables(pos):
    half = QK_ROPE_DIM // 2
    inv_freq = ROPE_THETA ** (-jnp.arange(half, dtype=F32) / half)
    ang = pos.astype(F32)[:, None] * inv_freq[None, :]
    cos, sin = jnp.cos(ang), jnp.sin(ang)
    return jnp.concatenate([cos, cos, cos, cos], axis=-1), jnp.concatenate([-sin, sin, -sin, sin], axis=-1)


def _swap_halves(w, half):
    return jnp.concatenate([w[..., half:], w[..., :half]], axis=-1)


def kernel(x_prompt, x_sample, state_conv, cache_ckv, cache_kpe, page_table, c_prompt, c_sample, w_ada, b_ada, g_pre, g_post, conv_w_in, conv_w, conv_w_out, kv_g_in, kv_w_down, kv_g_latent, kv_w_up, mla_w_in, mla_g_q, mla_w_uq, mla_w_out):
    nb, seq, d = x_prompt.shape
    nd, t_dec, _ = x_sample.shape
    depth = w_ada.shape[0]
    n_a = conv_w_in.shape[0]
    assert depth == 2 and n_a == 1 and mla_w_in.shape[0] == 1
    lat = kv_g_latent.shape[0]
    rope = QK_ROPE_DIM
    half = rope // 2
    past_len = page_table.shape[1] * cache_ckv.shape[1]

    n_c = nb + nd
    n_c_pad = -(-n_c // 8) * 8
    c_all = jnp.concatenate([c_prompt, c_sample, jnp.zeros((n_c_pad - n_c, d), F32)], axis=0)
    mod = _ada(c_all, w_ada, b_ada)
    mod_p = [mod[l, :nb].reshape(nb, 1, 3 * d) for l in range(depth)]
    mod_s = [mod[l, nb:n_c].reshape(nd, 1, 3 * d) for l in range(depth)]

    w_in0 = conv_w_in[0].astype(BF16)
    w_out0 = conv_w_out[0].astype(BF16)
    y_p, conv_p = _conv_in(x_prompt, mod_p[0], g_pre[0], None, w_in0, conv_w[0], sample=False)
    xp = _out_proj(y_p, x_prompt, mod_p[0], g_post[0], w_out0, name="out_proj_conv_prompt")
    y_s, conv_s = _conv_in(x_sample, mod_s[0], g_pre[0], state_conv[0], w_in0, conv_w[0], sample=True)
    xs = _out_proj(y_s, x_sample, mod_s[0], g_post[0], w_out0, name="out_proj_conv_sample")

    w_kpe = kv_w_down[:, lat:]
    w_kpe_sw = _swap_halves(w_kpe, half)
    w_down_ext = jnp.concatenate([kv_w_down[:, :lat], w_kpe, w_kpe, w_kpe_sw, w_kpe_sw], axis=1).astype(BF16)
    qr = mla_g_q.shape[1]
    w_uq = mla_w_uq[0].reshape(qr, N_HEADS, QK_NOPE_DIM + rope)
    w_uq_pe = w_uq[:, :, QK_NOPE_DIM:]
    w_uq_ext = jnp.concatenate(
        [w_uq[:, :, :QK_NOPE_DIM].reshape(qr, -1), w_uq_pe.reshape(qr, -1),
         _swap_halves(w_uq_pe, half).reshape(qr, -1)], axis=1).astype(BF16)
    w_uk_t = jnp.transpose(kv_w_up[:, :, :QK_NOPE_DIM], (1, 2, 0)).astype(BF16)
    w_uv = jnp.transpose(kv_w_up[:, :, QK_NOPE_DIM:], (1, 0, 2)).astype(BF16)
    w_uv_all = kv_w_up[:, :, QK_NOPE_DIM:].reshape(lat, -1).astype(BF16)
    w_in1 = mla_w_in[0].astype(BF16)
    w_out1 = mla_w_out[0].astype(BF16)

    cos_p, sin_p = _rope_tables(jnp.arange(seq))
    cos_s, sin_s = _rope_tables(past_len + jnp.arange(t_dec))
    reps = 256 // t_dec
    cos_s, sin_s = jnp.tile(cos_s, (reps, 1)), jnp.tile(sin_s, (reps, 1))

    ckv_p, kpe_p, kcat_p, qcat_p, z_p = _mla_in(
        xp, mod_p[1], g_pre[1], kv_g_in, w_down_ext, kv_g_latent, cos_p, sin_p, w_in1, mla_g_q[0], w_uq_ext,
        w_uk_t, sample=False)
    y_p = _attn_prompt(qcat_p, kcat_p, z_p, w_uv, nseq=nb)
    xp = _out_proj(y_p, xp, mod_p[1], g_post[1], w_out1, name="out_proj_mla_prompt")

    ckv_s, kpe_s, qlat_s, qpe_s, z_s = _mla_in(
        xs, mod_s[1], g_pre[1], kv_g_in, w_down_ext, kv_g_latent, cos_s, sin_s, w_in1, mla_g_q[0], w_uq_ext,
        w_uk_t, sample=True)
    cache_kpe_t = jnp.swapaxes(cache_kpe, 1, 2)
    y_s = _attn_sample(qlat_s, qpe_s, cache_ckv, cache_kpe_t, page_table, ckv_s, kpe_s, z_s, w_uv_all)
    xs = _out_proj(y_s, xs, mod_s[1], g_post[1], w_out1, name="out_proj_mla_sample")

    return (xp, xs, conv_p[None], conv_s[None],
            ckv_p.reshape(nb, seq, lat), kpe_p.reshape(nb, seq, rope),
            ckv_s.reshape(nd, t_dec, lat), kpe_s.reshape(nd, t_dec, rope))
```

```python
import functools

import jax
import jax.numpy as jnp
from jax import lax
from jax.experimental import pallas as pl
from jax.experimental.pallas import tpu as pltpu

F32 = jnp.float32
BF16 = jnp.bfloat16
RMS_EPS = 1e-6
ROPE_THETA = 10000.0
MIB = 1024 * 1024
NEG_INF = float("-inf")

N_HEADS = 16
QK_NOPE_DIM = 128
QK_ROPE_DIM = 64
V_HEAD_DIM = 128
SOFTMAX_SCALE = (QK_NOPE_DIM + QK_ROPE_DIM) ** -0.5


def _dot(a, b):
    return jnp.dot(a, b, preferred_element_type=F32)


def _dot_nt(a, b):
    return lax.dot_general(a, b, (((1,), (1,)), ((), ())), preferred_element_type=F32)


def _silu(x):
    return x * jax.nn.sigmoid(x)


def _unit_rms(x):
    return x * lax.rsqrt(jnp.mean(x * x, axis=-1, keepdims=True) + RMS_EPS)


def _params(semantics, vmem_mib):
    return pltpu.CompilerParams(dimension_semantics=semantics, vmem_limit_bytes=vmem_mib * MIB)


def _resident(block_shape, index_map):
    return pl.BlockSpec(block_shape, index_map, pipeline_mode=pl.Buffered(1))


def _ada_body(c_ref, w_ref, b_ref, o_ref):
    a = _silu(c_ref[...]).astype(BF16)
    o_ref[0] = _dot(a, w_ref[0].astype(BF16)) + b_ref[0]


def _ada(c_all, w_ada, b_ada):
    depth, d, d3 = w_ada.shape
    m = c_all.shape[0]
    tn = 1024
    return pl.pallas_call(
        _ada_body,
        grid=(depth, d3 // tn),
        in_specs=[
            pl.BlockSpec((m, d), lambda l, j: (0, 0)),
            pl.BlockSpec((1, d, tn), lambda l, j: (l, 0, j)),
            pl.BlockSpec((1, 1, tn), lambda l, j: (l, 0, j)),
        ],
        out_specs=pl.BlockSpec((1, m, tn), lambda l, j: (l, 0, j)),
        out_shape=jax.ShapeDtypeStruct((depth, m, d3), F32),
        compiler_params=_params(("arbitrary", "arbitrary"), 40),
        name="ada",
    )(c_all, w_ada, b_ada.reshape(depth, 1, d3))


def _conv_in_body(*refs, sample, tiles_per_seq):
    if sample:
        (x_ref, shift_ref, scale_ref, gpre_ref, st_ref, wb_ref, wc_ref, wv_ref, wz_ref, wconv_ref,
         y_ref, cs_ref, h_scr) = refs
    else:
        (x_ref, shift_ref, scale_ref, gpre_ref, wb_ref, wc_ref, wv_ref, wz_ref, wconv_ref,
         y_ref, cs_ref, h_scr, carry_scr) = refs
    i = pl.program_id(0)
    j = pl.program_id(1)
    g, r, d = x_ref.shape
    rows = g * r

    @pl.when(j == 0)
    def _():
        xn = _unit_rms(x_ref[...])
        h = xn * gpre_ref[...][None] * (1.0 + scale_ref[...]) + shift_ref[...]
        h_scr[...] = h.reshape(rows, d).astype(BF16)

    h = h_scr[...]
    bg = _dot(h, wb_ref[...])
    cg = _dot(h, wc_ref[...])
    v = _dot(h, wv_ref[...])
    z = _dot(h, wz_ref[...])
    u = cg * v
    tn = u.shape[1]
    t = lax.broadcasted_iota(jnp.int32, (rows, tn), 0)
    if sample:
        t = t & (r - 1)
        st = st_ref[...]
        s0 = jnp.broadcast_to(st[:, 0:1, :], (g, r, tn)).reshape(rows, tn)
        s1 = jnp.broadcast_to(st[:, 1:2, :], (g, r, tn)).reshape(rows, tn)
    else:
        @pl.when(i % tiles_per_seq == 0)
        def _():
            carry_scr[j] = jnp.zeros(carry_scr.shape[1:], F32)

        carry = carry_scr[j]
        s0 = carry[6:7, :]
        s1 = carry[7:8, :]
    p1 = jnp.where(t == 0, s1, pltpu.roll(u, 1, 0))
    p2 = jnp.where(t == 0, s0, jnp.where(t == 1, s1, pltpu.roll(u, 2, 0)))
    w = wconv_ref[...]
    conv = w[0:1, :] * p2 + w[1:2, :] * p1 + w[2:3, :] * u
    y_ref[...] = (bg * conv * _silu(z)).astype(BF16)
    if sample:
        cs_ref[...] = u.reshape(g, r, tn)[:, r - 2:r, :]
    else:
        tail = u[rows - 8:rows, :]
        carry_scr[j] = tail
        cs_ref[0] = tail


def _conv_in(x, mod, g_pre, state, w_in, w_conv, *, sample):
    nseq, t, d = x.shape
    dc = w_conv.shape[1]
    tn = 512
    nj = dc // tn
    tm = 512
    if sample:
        g, r = tm // t, t
        n_i = nseq // g
        tps = 1
        x_spec = pl.BlockSpec((g, r, d), lambda i, j: (i, 0, 0))
        mod_spec = lambda k: pl.BlockSpec((g, 1, d), lambda i, j: (i, 0, k))
    else:
        g, r = 1, tm
        tps = t // tm
        n_i = nseq * tps
        x_spec = pl.BlockSpec((1, tm, d), lambda i, j: (i // tps, i % tps, 0))
        mod_spec = lambda k: pl.BlockSpec((1, 1, d), lambda i, j: (i // tps, 0, k))
    w_spec = lambda s: pl.BlockSpec((d, tn), lambda i, j: (0, s * nj + j))
    in_specs = [x_spec, mod_spec(0), mod_spec(1), pl.BlockSpec((1, d), lambda i, j: (0, 0))]
    args = [x, mod, mod, g_pre.reshape(1, d)]
    scratch = [pltpu.VMEM((g * r, d), BF16)]
    if sample:
        in_specs.append(pl.BlockSpec((g, 2, tn), lambda i, j: (i, 0, j)))
        args.append(state)
        cs_shape = jax.ShapeDtypeStruct((nseq, 2, dc), F32)
        cs_spec = pl.BlockSpec((g, 2, tn), lambda i, j: (i, 0, j))
    else:
        cs_shape = jax.ShapeDtypeStruct((n_i, 8, dc), F32)
        cs_spec = pl.BlockSpec((1, 8, tn), lambda i, j: (i, 0, j))
        scratch.append(pltpu.VMEM((nj, 8, tn), F32))
    in_specs += [w_spec(0), w_spec(1), w_spec(2), w_spec(3), pl.BlockSpec((3, tn), lambda i, j: (0, j))]
    args += [w_in, w_in, w_in, w_in, w_conv]
    y, cs = pl.pallas_call(
        functools.partial(_conv_in_body, sample=sample, tiles_per_seq=tps),
        grid=(n_i, nj),
        in_specs=in_specs,
        out_specs=[pl.BlockSpec((g * r, tn), lambda i, j: (i, j)), cs_spec],
        out_shape=[jax.ShapeDtypeStruct((nseq * t, dc), BF16), cs_shape],
        scratch_shapes=scratch,
        compiler_params=_params(("arbitrary", "arbitrary"), 56),
        name="conv_in_sample" if sample else "conv_in_prompt",
    )(*args)
    if not sample:
        cs = cs.reshape(nseq, tps, 8, dc)[:, tps - 1, 6:8, :]
    return y, cs


def _out_proj_body(y_ref, x_ref, gate_ref, gpost_ref, w_ref, o_ref):
    g, r, d = x_ref.shape
    o = _dot(y_ref[...].astype(BF16), w_ref[...])
    o = _unit_rms(o) * gpost_ref[...]
    o_ref[...] = x_ref[...] + gate_ref[...] * o.reshape(g, r, d)


def _out_proj(y, x, mod, g_post, w_out, *, name):
    nseq, t, d = x.shape
    tm = 512
    if t >= tm:
        g, r = 1, tm
        tps = t // tm
        x_spec = pl.BlockSpec((1, tm, d), lambda i: (i // tps, i % tps, 0))
        gate_spec = pl.BlockSpec((1, 1, d), lambda i: (i // tps, 0, 2))
    else:
        g, r = tm // t, t
        tps = 1
        x_spec = pl.BlockSpec((g, r, d), lambda i: (i, 0, 0))
        gate_spec = pl.BlockSpec((g, 1, d), lambda i: (i, 0, 2))
    n_i = nseq * t // tm
    return pl.pallas_call(
        _out_proj_body,
        grid=(n_i,),
        in_specs=[
            pl.BlockSpec((tm, y.shape[1]), lambda i: (i, 0)),
            x_spec,
            gate_spec,
            pl.BlockSpec((1, d), lambda i: (0, 0)),
            _resident(w_out.shape, lambda i: (0, 0)),
        ],
        out_specs=x_spec,
        out_shape=jax.ShapeDtypeStruct(x.shape, F32),
        compiler_params=_params(("arbitrary",), 48),
        name=name,
    )(y, x, mod, g_post.reshape(1, d), w_out)


def _mla_in_body(x_ref, shift_ref, scale_ref, gpre_ref, gkv_ref, wdown_ref, glat_ref, cos_ref, sin_ref,
                 win_ref, gq_ref, wuq_ref, wuk_ref, *out_refs, sample):
    if sample:
        ckv_ref, kpe_ref, qlat_ref, qpe_ref, z_ref = out_refs
    else:
        ckv_ref, kpe_ref, kcat_ref, qcat_ref, z_ref = out_refs
    g, r, d = x_ref.shape
    rows = g * r
    lat = glat_ref.shape[1]
    rope = QK_ROPE_DIM
    xn = _unit_rms(x_ref[...])
    cos = cos_ref[...]
    sin = sin_ref[...]

    a = (xn * gkv_ref[...][None]).reshape(rows, d).astype(BF16)
    raw = _dot(a, wdown_ref[...])
    ckv = _unit_rms(raw[:, :lat]) * glat_ref[...]
    kpe2 = raw[:, lat:lat + 128] * cos + raw[:, lat + 128:lat + 256] * sin
    ckv_ref[...] = ckv
    kpe_ref[...] = kpe2[:, :rope]
    if not sample:
        kcat_ref[:, :lat] = ckv.astype(BF16)
        kcat_ref[:, lat:] = kpe2.astype(BF16)

    h = xn * gpre_ref[...][None] * (1.0 + scale_ref[...]) + shift_ref[...]
    qz = _dot(h.reshape(rows, d).astype(BF16), win_ref[...])
    qr = gq_ref.shape[1]
    z_ref[...] = qz[:, qr:]
    qa = (_unit_rms(qz[:, :qr]) * gq_ref[...]).astype(BF16)
    q = _dot(qa, wuq_ref[...]) * SOFTMAX_SCALE
    n_nope = N_HEADS * QK_NOPE_DIM
    n_rope = N_HEADS * rope
    for hd in range(N_HEADS):
        qn = q[:, hd * QK_NOPE_DIM:(hd + 1) * QK_NOPE_DIM].astype(BF16)
        ql = _dot(qn, wuk_ref[hd])
        if sample:
            qlat_ref[:, hd * r:(hd + 1) * r, :] = ql.reshape(g, r, lat)
        else:
            qcat_ref[hd, :, :lat] = ql.astype(BF16)
    lane = lax.broadcasted_iota(jnp.int32, (rows, 128), 1)
    for pr in range(N_HEADS // 2):
        lo = n_nope + pr * 128
        pe = q[:, lo:lo + 128] * cos + q[:, lo + n_rope:lo + n_rope + 128] * sin
        for k in range(2):
            hd = 2 * pr + k
            if sample:
                qpe_ref[:, hd * r:(hd + 1) * r, :] = pe[:, k * rope:(k + 1) * rope].reshape(g, r, rope)
            else:
                own = (lane < rope) if k == 0 else (lane >= rope)
                qcat_ref[hd, :, lat:] = jnp.where(own, pe, 0.0).astype(BF16)


def _mla_in(x, mod, g_pre, kv_g_in, w_down_ext, kv_g_latent, cos_tab, sin_tab, w_in, g_q, w_uq_ext, w_uk_t,
            *, sample):
    nseq, t, d = x.shape
    n_tok = nseq * t
    lat = kv_g_latent.shape[0]
    rope = QK_ROPE_DIM
    d_attn = w_in.shape[1] - g_q.shape[0]
    tm = 256
    n_i = n_tok // tm
    row_spec = lambda w: pl.BlockSpec((tm, w), lambda i: (i, 0))
    if sample:
        g, r = tm // t, t
        x_spec = pl.BlockSpec((g, r, d), lambda i: (i, 0, 0))
        mod_spec = lambda k: pl.BlockSpec((g, 1, d), lambda i: (i, 0, k))
        tab_spec = pl.BlockSpec((tm, 128), lambda i: (0, 0))
        qk_shapes = [jax.ShapeDtypeStruct((nseq, N_HEADS * t, lat), F32),
                     jax.ShapeDtypeStruct((nseq, N_HEADS * t, rope), F32)]
        qk_specs = [pl.BlockSpec((g, N_HEADS * t, lat), lambda i: (i, 0, 0)),
                    pl.BlockSpec((g, N_HEADS * t, rope), lambda i: (i, 0, 0))]
    else:
        tps = t // tm
        x_spec = pl.BlockSpec((1, tm, d), lambda i: (i // tps, i % tps, 0))
        mod_spec = lambda k: pl.BlockSpec((1, 1, d), lambda i: (i // tps, 0, k))
        tab_spec = pl.BlockSpec((tm, 128), lambda i: (i % tps, 0))
        qk_shapes = [jax.ShapeDtypeStruct((n_tok, lat + 128), BF16),
                     jax.ShapeDtypeStruct((N_HEADS, n_tok, lat + 128), BF16)]
        qk_specs = [row_spec(lat + 128), pl.BlockSpec((N_HEADS, tm, lat + 128), lambda i: (0, i, 0))]
    vec = lambda n: pl.BlockSpec((1, n), lambda i: (0, 0))
    return pl.pallas_call(
        functools.partial(_mla_in_body, sample=sample),
        grid=(n_i,),
        in_specs=[
            x_spec, mod_spec(0), mod_spec(1), vec(d), vec(d),
            _resident(w_down_ext.shape, lambda i: (0, 0)),
            vec(lat), tab_spec, tab_spec,
            _resident(w_in.shape, lambda i: (0, 0)),
            vec(g_q.shape[0]),
            _resident(w_uq_ext.shape, lambda i: (0, 0)),
            _resident(w_uk_t.shape, lambda i: (0, 0, 0)),
        ],
        out_specs=[row_spec(lat), row_spec(rope)] + qk_specs + [row_spec(d_attn)],
        out_shape=[jax.ShapeDtypeStruct((n_tok, lat), F32), jax.ShapeDtypeStruct((n_tok, rope), F32)]
        + qk_shapes + [jax.ShapeDtypeStruct((n_tok, d_attn), F32)],
        compiler_params=_params(("arbitrary",), 58),
        name="mla_in_sample" if sample else "mla_in_prompt",
    )(x, mod, mod, g_pre.reshape(1, d), kv_g_in.reshape(1, d), w_down_ext, kv_g_latent.reshape(1, lat),
      cos_tab, sin_tab, w_in, g_q.reshape(1, -1), w_uq_ext, w_uk_t)


LANES = 128


def _softmax_step(s, v, m_scr, l_scr, acc_scr):
    n_chunks = s.shape[1] // LANES
    chunks = [s[:, c * LANES:(c + 1) * LANES] for c in range(n_chunks)]
    m_cur = chunks[0]
    for ch in chunks[1:]:
        m_cur = jnp.maximum(m_cur, ch)
    m_prev = m_scr[...]
    m_new = jnp.maximum(m_prev, jnp.max(m_cur, axis=1, keepdims=True))
    alpha = jnp.exp(m_prev - m_new)
    ps = [jnp.exp(ch - m_new) for ch in chunks]
    p_sum = ps[0]
    for p in ps[1:]:
        p_sum = p_sum + p
    l_scr[...] = alpha * l_scr[...] + p_sum
    m_scr[...] = m_new
    pv = _dot(jnp.concatenate(ps, axis=1).astype(BF16), v)
    for c in range(acc_scr.shape[1] // LANES):
        sl = slice(c * LANES, (c + 1) * LANES)
        acc_scr[:, sl] = alpha * acc_scr[:, sl] + pv[:, sl]


def _softmax_init(m_scr, l_scr, acc_scr):
    m_scr[...] = jnp.full(m_scr.shape, NEG_INF, F32)
    l_scr[...] = jnp.zeros(l_scr.shape, F32)
    acc_scr[...] = jnp.zeros(acc_scr.shape, F32)


def _softmax_inv_denominator(l_scr):
    return 1.0 / jnp.sum(l_scr[...], axis=1, keepdims=True)


def _attn_prompt_body(q_ref, k_ref, z_ref, wuv_ref, y_ref, m_scr, l_scr, acc_scr, *, tk, tiles_per_seq):
    nh, tq, dk = q_ref.shape
    rows, lat = acc_scr.shape
    qi = pl.program_id(0) % tiles_per_seq
    q = q_ref[...].reshape(rows, dk)
    _softmax_init(m_scr, l_scr, acc_scr)

    def load_keys(kj):
        return k_ref[pl.ds(pl.multiple_of(kj * tk, tk), tk), :]

    def full_block(kj, carry):
        k = load_keys(kj)
        _softmax_step(_dot_nt(q, k), k[:, :lat], m_scr, l_scr, acc_scr)
        return carry

    n_full = (qi * tq) // tk
    lax.fori_loop(0, n_full, full_block, 0)

    k = load_keys(n_full)
    q_pos = qi * tq + (lax.broadcasted_iota(jnp.int32, (rows, tk), 0) & (tq - 1))
    k_pos = n_full * tk + lax.broadcasted_iota(jnp.int32, (rows, tk), 1)
    s = jnp.where(k_pos <= q_pos, _dot_nt(q, k), NEG_INF)
    _softmax_step(s, k[:, :lat], m_scr, l_scr, acc_scr)

    inv_l = _softmax_inv_denominator(l_scr)
    vd = wuv_ref.shape[2]
    for hd in range(nh):
        ol = (acc_scr[hd * tq:(hd + 1) * tq, :] * inv_l[hd * tq:(hd + 1) * tq, :]).astype(BF16)
        o = _dot(ol, wuv_ref[hd])
        y_ref[:, hd * vd:(hd + 1) * vd] = (o * _silu(z_ref[:, hd * vd:(hd + 1) * vd])).astype(BF16)


def _attn_prompt(qcat, kcat, z, w_uv, *, nseq):
    nh, n_tok, dk = qcat.shape
    lat = w_uv.shape[1]
    t = n_tok // nseq
    tq, tk = 128, 512
    tps = t // tq
    d_attn = z.shape[1]
    return pl.pallas_call(
        functools.partial(_attn_prompt_body, tk=tk, tiles_per_seq=tps),
        grid=(n_tok // tq,),
        in_specs=[
            pl.BlockSpec((nh, tq, dk), lambda i: (0, i, 0)),
            pl.BlockSpec((t, dk), lambda i: (i // tps, 0)),
            pl.BlockSpec((tq, d_attn), lambda i: (i, 0)),
            _resident(w_uv.shape, lambda i: (0, 0, 0)),
        ],
        out_specs=pl.BlockSpec((tq, d_attn), lambda i: (i, 0)),
        out_shape=jax.ShapeDtypeStruct((n_tok, d_attn), BF16),
        scratch_shapes=[pltpu.VMEM((nh * tq, LANES), F32), pltpu.VMEM((nh * tq, LANES), F32),
                        pltpu.VMEM((nh * tq, lat), F32)],
        compiler_params=_params(("arbitrary",), 56),
        name="attn_prompt",
    )(qcat, kcat, z, w_uv)


def _attn_sample_body(pt_ref, q_ref, qp_ref, ckvn_ref, kpen_ref, z_ref, wuv_ref, ck_hbm, kpt_hbm, y_ref,
                      m_scr, l_scr, acc_scr, k_buf, kpt_buf, sem, *, n_pg, n_chunks):
    b = pl.program_id(0)
    page = ck_hbm.shape[1]
    rows, lat = acc_scr.shape
    t_new = ckvn_ref.shape[0]
    assert n_chunks % 2 == 0

    def chunk_copies(seq, c):
        slot = c % 2
        copies = []
        for k in range(n_pg):
            pg = pt_ref[seq, c * n_pg + k]
            copies.append(pltpu.make_async_copy(
                ck_hbm.at[pg], k_buf.at[slot, pl.ds(k * page, page), :], sem.at[0, slot]))
            copies.append(pltpu.make_async_copy(
                kpt_hbm.at[pg], kpt_buf.at[slot, :, pl.ds(k * page, page)], sem.at[1, slot]))
        return copies

    def start_chunk(seq, c):
        for cp in chunk_copies(seq, c):
            cp.start()

    @pl.when(b == 0)
    def _():
        start_chunk(b, 0)

    _softmax_init(m_scr, l_scr, acc_scr)
    q = q_ref[0].astype(BF16)
    qp = qp_ref[0].astype(BF16)
    for c in range(n_chunks):
        if c + 1 < n_chunks:
            start_chunk(b, c + 1)
        else:
            @pl.when(b + 1 < pl.num_programs(0))
            def _():
                start_chunk(b + 1, 0)
        for cp in chunk_copies(b, c):
            cp.wait()
        kk = k_buf[c % 2].astype(BF16)
        kpt = kpt_buf[c % 2].astype(BF16)
        _softmax_step(_dot_nt(q, kk) + _dot(qp, kpt), kk, m_scr, l_scr, acc_scr)

    pad = page - t_new
    kn = jnp.concatenate([ckvn_ref[...], jnp.zeros((pad, lat), F32)], axis=0).astype(BF16)
    kpn = jnp.concatenate([kpen_ref[...], jnp.zeros((pad, kpen_ref.shape[1]), F32)], axis=0).astype(BF16)
    s = _dot_nt(q, kn) + _dot_nt(qp, kpn)
    t_q = lax.broadcasted_iota(jnp.int32, (rows, page), 0) & (t_new - 1)
    t_k = lax.broadcasted_iota(jnp.int32, (rows, page), 1)
    _softmax_step(jnp.where(t_k <= t_q, s, NEG_INF), kn, m_scr, l_scr, acc_scr)

    ol = (acc_scr[...] * _softmax_inv_denominator(l_scr)).astype(BF16)
    o_all = _dot(ol, wuv_ref[...])
    vd = wuv_ref.shape[1] // N_HEADS
    for hd in range(N_HEADS):
        o = o_all[hd * t_new:(hd + 1) * t_new, hd * vd:(hd + 1) * vd]
        y_ref[:, hd * vd:(hd + 1) * vd] = o * _silu(z_ref[:, hd * vd:(hd + 1) * vd])


def _attn_sample(qlat, qpe, cache_ckv, cache_kpe_t, page_table, ckv_new, kpe_new, z, w_uv_all):
    nseq, rows, lat = qlat.shape
    rope = qpe.shape[2]
    page = cache_ckv.shape[1]
    n_pages = page_table.shape[1]
    t_new = ckv_new.shape[0] // nseq
    d_attn = z.shape[1]
    n_pg = 16
    n_chunks = n_pages // n_pg

    grid_spec = pltpu.PrefetchScalarGridSpec(
        num_scalar_prefetch=1,
        grid=(nseq,),
        in_specs=[
            pl.BlockSpec((1, rows, lat), lambda b, pt: (b, 0, 0)),
            pl.BlockSpec((1, rows, rope), lambda b, pt: (b, 0, 0)),
            pl.BlockSpec((t_new, lat), lambda b, pt: (b, 0)),
            pl.BlockSpec((t_new, rope), lambda b, pt: (b, 0)),
            pl.BlockSpec((t_new, d_attn), lambda b, pt: (b, 0)),
            _resident(w_uv_all.shape, lambda b, pt: (0, 0)),
            pl.BlockSpec(memory_space=pl.ANY),
            pl.BlockSpec(memory_space=pl.ANY),
        ],
        out_specs=pl.BlockSpec((t_new, d_attn), lambda b, pt: (b, 0)),
        scratch_shapes=[pltpu.VMEM((rows, LANES), F32), pltpu.VMEM((rows, LANES), F32),
                        pltpu.VMEM((rows, lat), F32),
                        pltpu.VMEM((2, n_pg * page, lat), F32), pltpu.VMEM((2, rope, n_pg * page), F32),
                        pltpu.SemaphoreType.DMA((2, 2))],
    )
    return pl.pallas_call(
        functools.partial(_attn_sample_body, n_pg=n_pg, n_chunks=n_chunks),
        grid_spec=grid_spec,
        out_shape=jax.ShapeDtypeStruct((nseq * t_new, d_attn), F32),
        compiler_params=_params(("arbitrary",), 48),
        name="attn_sample",
    )(page_table, qlat, qpe, ckv_new, kpe_new, z, w_uv_all, cache_ckv, cache_kpe_t)


def _rope_tables(pos):
    half = QK_ROPE_DIM // 2
    inv_freq = ROPE_THETA ** (-jnp.arange(half, dtype=F32) / half)
    ang = pos.astype(F32)[:, None] * inv_freq[None, :]
    cos, sin = jnp.cos(ang), jnp.sin(ang)
    return jnp.concatenate([cos, cos, cos, cos], axis=-1), jnp.concatenate([-sin, sin, -sin, sin], axis=-1)


def _swap_halves(w, half):
    return jnp.concatenate([w[..., half:], w[..., :half]], axis=-1)


def kernel(x_prompt, x_sample, state_conv, cache_ckv, cache_kpe, page_table, c_prompt, c_sample, w_ada, b_ada, g_pre, g_post, conv_w_in, conv_w, conv_w_out, kv_g_in, kv_w_down, kv_g_latent, kv_w_up, mla_w_in, mla_g_q, mla_w_uq, mla_w_out):
    nb, seq, d = x_prompt.shape
    nd, t_dec, _ = x_sample.shape
    depth = w_ada.shape[0]
    n_a = conv_w_in.shape[0]
    assert depth == 2 and n_a == 1 and mla_w_in.shape[0] == 1
    lat = kv_g_latent.shape[0]
    rope = QK_ROPE_DIM
    half = rope // 2
    past_len = page_table.shape[1] * cache_ckv.shape[1]

    n_c = nb + nd
    n_c_pad = -(-n_c // 8) * 8
    c_all = jnp.concatenate([c_prompt, c_sample, jnp.zeros((n_c_pad - n_c, d), F32)], axis=0)
    mod = _ada(c_all, w_ada, b_ada)
    mod_p = [mod[l, :nb].reshape(nb, 1, 3 * d) for l in range(depth)]
    mod_s = [mod[l, nb:n_c].reshape(nd, 1, 3 * d) for l in range(depth)]

    w_in0 = conv_w_in[0].astype(BF16)
    w_out0 = conv_w_out[0].astype(BF16)
    y_p, conv_p = _conv_in(x_prompt, mod_p[0], g_pre[0], None, w_in0, conv_w[0], sample=False)
    xp = _out_proj(y_p, x_prompt, mod_p[0], g_post[0], w_out0, name="out_proj_conv_prompt")
    y_s, conv_s = _conv_in(x_sample, mod_s[0], g_pre[0], state_conv[0], w_in0, conv_w[0], sample=True)
    xs = _out_proj(y_s, x_sample, mod_s[0], g_post[0], w_out0, name="out_proj_conv_sample")

    w_kpe = kv_w_down[:, lat:]
    w_kpe_sw = _swap_halves(w_kpe, half)
    w_down_ext = jnp.concatenate([kv_w_down[:, :lat], w_kpe, w_kpe, w_kpe_sw, w_kpe_sw], axis=1).astype(BF16)
    qr = mla_g_q.shape[1]
    w_uq = mla_w_uq[0].reshape(qr, N_HEADS, QK_NOPE_DIM + rope)
    w_uq_pe = w_uq[:, :, QK_NOPE_DIM:]
    w_uq_ext = jnp.concatenate(
        [w_uq[:, :, :QK_NOPE_DIM].reshape(qr, -1), w_uq_pe.reshape(qr, -1),
         _swap_halves(w_uq_pe, half).reshape(qr, -1)], axis=1).astype(BF16)
    w_uk_t = jnp.transpose(kv_w_up[:, :, :QK_NOPE_DIM], (1, 2, 0)).astype(BF16)
    w_uv = jnp.transpose(kv_w_up[:, :, QK_NOPE_DIM:], (1, 0, 2)).astype(BF16)
    w_uv_all = kv_w_up[:, :, QK_NOPE_DIM:].reshape(lat, -1).astype(BF16)
    w_in1 = mla_w_in[0].astype(BF16)
    w_out1 = mla_w_out[0].astype(BF16)

    cos_p, sin_p = _rope_tables(jnp.arange(seq))
    cos_s, sin_s = _rope_tables(past_len + jnp.arange(t_dec))
    reps = 256 // t_dec
    cos_s, sin_s = jnp.tile(cos_s, (reps, 1)), jnp.tile(sin_s, (reps, 1))

    ckv_p, kpe_p, kcat_p, qcat_p, z_p = _mla_in(
        xp, mod_p[1], g_pre[1], kv_g_in, w_down_ext, kv_g_latent, cos_p, sin_p, w_in1, mla_g_q[0], w_uq_ext,
        w_uk_t, sample=False)
    y_p = _attn_prompt(qcat_p, kcat_p, z_p, w_uv, nseq=nb)
    xp = _out_proj(y_p, xp, mod_p[1], g_post[1], w_out1, name="out_proj_mla_prompt")

    ckv_s, kpe_s, qlat_s, qpe_s, z_s = _mla_in(
        xs, mod_s[1], g_pre[1], kv_g_in, w_down_ext, kv_g_latent, cos_s, sin_s, w_in1, mla_g_q[0], w_uq_ext,
        w_uk_t, sample=True)
    cache_kpe_t = jnp.swapaxes(cache_kpe, 1, 2)
    y_s = _attn_sample(qlat_s, qpe_s, cache_ckv, cache_kpe_t, page_table, ckv_s, kpe_s, z_s, w_uv_all)
    xs = _out_proj(y_s, xs, mod_s[1], g_post[1], w_out1, name="out_proj_mla_sample")

    return (xp, xs, conv_p[None], conv_s[None],
            ckv_p.reshape(nb, seq, lat), kpe_p.reshape(nb, seq, rope),
            ckv_s.reshape(nd, t_dec, lat), kpe_s.reshape(nd, t_dec, rope))
```

```python
import functools

import jax
import jax.numpy as jnp
from jax import lax
from jax.experimental import pallas as pl
from jax.experimental.pallas import tpu as pltpu

F32 = jnp.float32
BF16 = jnp.bfloat16
RMS_EPS = 1e-6
ROPE_THETA = 10000.0
MIB = 1024 * 1024
NEG_INF = float("-inf")

N_HEADS = 16
QK_NOPE_DIM = 128
QK_ROPE_DIM = 64
V_HEAD_DIM = 128
SOFTMAX_SCALE = (QK_NOPE_DIM + QK_ROPE_DIM) ** -0.5


def _dot(a, b):
    return jnp.dot(a, b, preferred_element_type=F32)


def _dot_nt(a, b):
    return lax.dot_general(a, b, (((1,), (1,)), ((), ())), preferred_element_type=F32)


def _silu(x):
    return x * jax.nn.sigmoid(x)


def _unit_rms(x):
    return x * lax.rsqrt(jnp.mean(x * x, axis=-1, keepdims=True) + RMS_EPS)


def _params(semantics, vmem_mib):
    return pltpu.CompilerParams(dimension_semantics=semantics, vmem_limit_bytes=vmem_mib * MIB)


def _resident(block_shape, index_map):
    return pl.BlockSpec(block_shape, index_map, pipeline_mode=pl.Buffered(1))


def _ada_body(c_ref, w_ref, b_ref, o_ref):
    a = _silu(c_ref[...]).astype(BF16)
    o_ref[0] = _dot(a, w_ref[0].astype(BF16)) + b_ref[0]


def _ada(c_all, w_ada, b_ada):
    depth, d, d3 = w_ada.shape
    m = c_all.shape[0]
    tn = 1024
    return pl.pallas_call(
        _ada_body,
        grid=(depth, d3 // tn),
        in_specs=[
            pl.BlockSpec((m, d), lambda l, j: (0, 0)),
            pl.BlockSpec((1, d, tn), lambda l, j: (l, 0, j)),
            pl.BlockSpec((1, 1, tn), lambda l, j: (l, 0, j)),
        ],
        out_specs=pl.BlockSpec((1, m, tn), lambda l, j: (l, 0, j)),
        out_shape=jax.ShapeDtypeStruct((depth, m, d3), F32),
        compiler_params=_params(("arbitrary", "arbitrary"), 40),
        name="ada",
    )(c_all, w_ada, b_ada.reshape(depth, 1, d3))


CONV_COL_SPLIT = 2


def _conv_in_body(*refs, sample, tiles_per_seq):
    if sample:
        (x_ref, shift_ref, scale_ref, gpre_ref, st_ref, wb_ref, wc_ref, wv_ref, wz_ref, wconv_ref,
         y_ref, cs_ref, h_scr) = refs
    else:
        (x_ref, shift_ref, scale_ref, gpre_ref, wb_ref, wc_ref, wv_ref, wz_ref, wconv_ref,
         y_ref, cs_ref, h_scr, carry_scr) = refs
    i = pl.program_id(0)
    j = pl.program_id(1)
    g, r, d = x_ref.shape
    rows = g * r

    @pl.when(j == 0)
    def _():
        xn = _unit_rms(x_ref[...])
        h = xn * gpre_ref[...][None] * (1.0 + scale_ref[...]) + shift_ref[...]
        h_scr[...] = h.reshape(rows, d).astype(BF16)

    if not sample:
        @pl.when(i % tiles_per_seq == 0)
        def _():
            carry_scr[j] = jnp.zeros(carry_scr.shape[1:], F32)

    h = h_scr[...]
    tn = y_ref.shape[1]
    tc = tn // CONV_COL_SPLIT
    for cb in range(CONV_COL_SPLIT):
        cols = slice(cb * tc, (cb + 1) * tc)
        bg = _dot(h, wb_ref[:, cols])
        cg = _dot(h, wc_ref[:, cols])
        v = _dot(h, wv_ref[:, cols])
        z = _dot(h, wz_ref[:, cols])
        u = cg * v
        t = lax.broadcasted_iota(jnp.int32, (rows, tc), 0)
        if sample:
            t = t & (r - 1)
            st = st_ref[:, :, cols]
            s0 = jnp.broadcast_to(st[:, 0:1, :], (g, r, tc)).reshape(rows, tc)
            s1 = jnp.broadcast_to(st[:, 1:2, :], (g, r, tc)).reshape(rows, tc)
        else:
            carry = carry_scr[j, :, cols]
            s0 = carry[6:7, :]
            s1 = carry[7:8, :]
        p1 = jnp.where(t == 0, s1, pltpu.roll(u, 1, 0))
        p2 = jnp.where(t == 0, s0, jnp.where(t == 1, s1, pltpu.roll(u, 2, 0)))
        w = wconv_ref[:, cols]
        conv = w[0:1, :] * p2 + w[1:2, :] * p1 + w[2:3, :] * u
        y_ref[:, cols] = (bg * conv * _silu(z)).astype(BF16)
        if sample:
            cs_ref[:, :, cols] = u.reshape(g, r, tc)[:, r - 2:r, :]
        else:
            tail = u[rows - 8:rows, :]
            carry_scr[j, :, cols] = tail
            cs_ref[0, :, cols] = tail


def _conv_in(x, mod, g_pre, state, w_in, w_conv, *, sample):
    nseq, t, d = x.shape
    dc = w_conv.shape[1]
    tn = 512
    nj = dc // tn
    tm = 512
    if sample:
        g, r = tm // t, t
        n_i = nseq // g
        tps = 1
        x_spec = pl.BlockSpec((g, r, d), lambda i, j: (i, 0, 0))
        mod_spec = lambda k: pl.BlockSpec((g, 1, d), lambda i, j: (i, 0, k))
    else:
        g, r = 1, tm
        tps = t // tm
        n_i = nseq * tps
        x_spec = pl.BlockSpec((1, tm, d), lambda i, j: (i // tps, i % tps, 0))
        mod_spec = lambda k: pl.BlockSpec((1, 1, d), lambda i, j: (i // tps, 0, k))
    w_spec = lambda s: pl.BlockSpec((d, tn), lambda i, j: (0, s * nj + j))
    in_specs = [x_spec, mod_spec(0), mod_spec(1), pl.BlockSpec((1, d), lambda i, j: (0, 0))]
    args = [x, mod, mod, g_pre.reshape(1, d)]
    scratch = [pltpu.VMEM((g * r, d), BF16)]
    if sample:
        in_specs.append(pl.BlockSpec((g, 2, tn), lambda i, j: (i, 0, j)))
        args.append(state)
        cs_shape = jax.ShapeDtypeStruct((nseq, 2, dc), F32)
        cs_spec = pl.BlockSpec((g, 2, tn), lambda i, j: (i, 0, j))
    else:
        cs_shape = jax.ShapeDtypeStruct((n_i, 8, dc), F32)
        cs_spec = pl.BlockSpec((1, 8, tn), lambda i, j: (i, 0, j))
        scratch.append(pltpu.VMEM((nj, 8, tn), F32))
    in_specs += [w_spec(0), w_spec(1), w_spec(2), w_spec(3), pl.BlockSpec((3, tn), lambda i, j: (0, j))]
    args += [w_in, w_in, w_in, w_in, w_conv]
    y, cs = pl.pallas_call(
        functools.partial(_conv_in_body, sample=sample, tiles_per_seq=tps),
        grid=(n_i, nj),
        in_specs=in_specs,
        out_specs=[pl.BlockSpec((g * r, tn), lambda i, j: (i, j)), cs_spec],
        out_shape=[jax.ShapeDtypeStruct((nseq * t, dc), BF16), cs_shape],
        scratch_shapes=scratch,
        compiler_params=_params(("arbitrary", "arbitrary"), 56),
        name="conv_in_sample" if sample else "conv_in_prompt",
    )(*args)
    if not sample:
        cs = cs.reshape(nseq, tps, 8, dc)[:, tps - 1, 6:8, :]
    return y, cs


def _out_proj_body(y_ref, x_ref, gate_ref, gpost_ref, w_ref, o_ref):
    g, r, d = x_ref.shape
    o = _dot(y_ref[...].astype(BF16), w_ref[...])
    o = _unit_rms(o) * gpost_ref[...]
    o_ref[...] = x_ref[...] + gate_ref[...] * o.reshape(g, r, d)


def _out_proj(y, x, mod, g_post, w_out, *, name):
    nseq, t, d = x.shape
    tm = 512
    if t >= tm:
        g, r = 1, tm
        tps = t // tm
        x_spec = pl.BlockSpec((1, tm, d), lambda i: (i // tps, i % tps, 0))
        gate_spec = pl.BlockSpec((1, 1, d), lambda i: (i // tps, 0, 2))
    else:
        g, r = tm // t, t
        tps = 1
        x_spec = pl.BlockSpec((g, r, d), lambda i: (i, 0, 0))
        gate_spec = pl.BlockSpec((g, 1, d), lambda i: (i, 0, 2))
    n_i = nseq * t // tm
    return pl.pallas_call(
        _out_proj_body,
        grid=(n_i,),
        in_specs=[
            pl.BlockSpec((tm, y.shape[1]), lambda i: (i, 0)),
            x_spec,
            gate_spec,
            pl.BlockSpec((1, d), lambda i: (0, 0)),
            _resident(w_out.shape, lambda i: (0, 0)),
        ],
        out_specs=x_spec,
        out_shape=jax.ShapeDtypeStruct(x.shape, F32),
        compiler_params=_params(("arbitrary",), 48),
        name=name,
    )(y, x, mod, g_post.reshape(1, d), w_out)


def _mla_in_body(x_ref, shift_ref, scale_ref, gpre_ref, gkv_ref, wdown_ref, glat_ref, cos_ref, sin_ref,
                 win_ref, gq_ref, wuq_ref, wuk_ref, *out_refs, sample):
    if sample:
        ckv_ref, kpe_ref, qlat_ref, qpe_ref, z_ref = out_refs
    else:
        ckv_ref, kpe_ref, kcat_ref, qcat_ref, z_ref = out_refs
    g, r, d = x_ref.shape
    rows = g * r
    lat = glat_ref.shape[1]
    rope = QK_ROPE_DIM
    xn = _unit_rms(x_ref[...])
    cos = cos_ref[...]
    sin = sin_ref[...]

    a = (xn * gkv_ref[...][None]).reshape(rows, d).astype(BF16)
    raw = _dot(a, wdown_ref[...])
    ckv = _unit_rms(raw[:, :lat]) * glat_ref[...]
    kpe2 = raw[:, lat:lat + 128] * cos + raw[:, lat + 128:lat + 256] * sin
    ckv_ref[...] = ckv
    kpe_ref[...] = kpe2[:, :rope]
    if not sample:
        kcat_ref[:, :lat] = ckv.astype(BF16)
        kcat_ref[:, lat:] = kpe2.astype(BF16)

    h = xn * gpre_ref[...][None] * (1.0 + scale_ref[...]) + shift_ref[...]
    qz = _dot(h.reshape(rows, d).astype(BF16), win_ref[...])
    qr = gq_ref.shape[1]
    z_ref[...] = qz[:, qr:]
    qa = (_unit_rms(qz[:, :qr]) * gq_ref[...]).astype(BF16)
    q = _dot(qa, wuq_ref[...]) * SOFTMAX_SCALE
    n_nope = N_HEADS * QK_NOPE_DIM
    n_rope = N_HEADS * rope
    for hd in range(N_HEADS):
        qn = q[:, hd * QK_NOPE_DIM:(hd + 1) * QK_NOPE_DIM].astype(BF16)
        ql = _dot(qn, wuk_ref[hd])
        if sample:
            qlat_ref[:, hd * r:(hd + 1) * r, :] = ql.reshape(g, r, lat)
        else:
            qcat_ref[hd, :, :lat] = ql.astype(BF16)
    lane = lax.broadcasted_iota(jnp.int32, (rows, 128), 1)
    for pr in range(N_HEADS // 2):
        lo = n_nope + pr * 128
        pe = q[:, lo:lo + 128] * cos + q[:, lo + n_rope:lo + n_rope + 128] * sin
        for k in range(2):
            hd = 2 * pr + k
            if sample:
                qpe_ref[:, hd * r:(hd + 1) * r, :] = pe[:, k * rope:(k + 1) * rope].reshape(g, r, rope)
            else:
                own = (lane < rope) if k == 0 else (lane >= rope)
                qcat_ref[hd, :, lat:] = jnp.where(own, pe, 0.0).astype(BF16)


def _mla_in(x, mod, g_pre, kv_g_in, w_down_ext, kv_g_latent, cos_tab, sin_tab, w_in, g_q, w_uq_ext, w_uk_t,
            *, sample):
    nseq, t, d = x.shape
    n_tok = nseq * t
    lat = kv_g_latent.shape[0]
    rope = QK_ROPE_DIM
    d_attn = w_in.shape[1] - g_q.shape[0]
    tm = 256
    n_i = n_tok // tm
    row_spec = lambda w: pl.BlockSpec((tm, w), lambda i: (i, 0))
    if sample:
        g, r = tm // t, t
        x_spec = pl.BlockSpec((g, r, d), lambda i: (i, 0, 0))
        mod_spec = lambda k: pl.BlockSpec((g, 1, d), lambda i: (i, 0, k))
        tab_spec = pl.BlockSpec((tm, 128), lambda i: (0, 0))
        qk_shapes = [jax.ShapeDtypeStruct((nseq, N_HEADS * t, lat), F32),
                     jax.ShapeDtypeStruct((nseq, N_HEADS * t, rope), F32)]
        qk_specs = [pl.BlockSpec((g, N_HEADS * t, lat), lambda i: (i, 0, 0)),
                    pl.BlockSpec((g, N_HEADS * t, rope), lambda i: (i, 0, 0))]
    else:
        tps = t // tm
        x_spec = pl.BlockSpec((1, tm, d), lambda i: (i // tps, i % tps, 0))
        mod_spec = lambda k: pl.BlockSpec((1, 1, d), lambda i: (i // tps, 0, k))
        tab_spec = pl.BlockSpec((tm, 128), lambda i: (i % tps, 0))
        qk_shapes = [jax.ShapeDtypeStruct((n_tok, lat + 128), BF16),
                     jax.ShapeDtypeStruct((N_HEADS, n_tok, lat + 128), BF16)]
        qk_specs = [row_spec(lat + 128), pl.BlockSpec((N_HEADS, tm, lat + 128), lambda i: (0, i, 0))]
    vec = lambda n: pl.BlockSpec((1, n), lambda i: (0, 0))
    return pl.pallas_call(
        functools.partial(_mla_in_body, sample=sample),
        grid=(n_i,),
        in_specs=[
            x_spec, mod_spec(0), mod_spec(1), vec(d), vec(d),
            _resident(w_down_ext.shape, lambda i: (0, 0)),
            vec(lat), tab_spec, tab_spec,
            _resident(w_in.shape, lambda i: (0, 0)),
            vec(g_q.shape[0]),
            _resident(w_uq_ext.shape, lambda i: (0, 0)),
            _resident(w_uk_t.shape, lambda i: (0, 0, 0)),
        ],
        out_specs=[row_spec(lat), row_spec(rope)] + qk_specs + [row_spec(d_attn)],
        out_shape=[jax.ShapeDtypeStruct((n_tok, lat), F32), jax.ShapeDtypeStruct((n_tok, rope), F32)]
        + qk_shapes + [jax.ShapeDtypeStruct((n_tok, d_attn), F32)],
        compiler_params=_params(("arbitrary",), 58),
        name="mla_in_sample" if sample else "mla_in_prompt",
    )(x, mod, mod, g_pre.reshape(1, d), kv_g_in.reshape(1, d), w_down_ext, kv_g_latent.reshape(1, lat),
      cos_tab, sin_tab, w_in, g_q.reshape(1, -1), w_uq_ext, w_uk_t)


LANES = 128


def _softmax_step(s, v, m_scr, l_scr, acc_scr):
    n_chunks = s.shape[1] // LANES
    chunks = [s[:, c * LANES:(c + 1) * LANES] for c in range(n_chunks)]
    m_cur = chunks[0]
    for ch in chunks[1:]:
        m_cur = jnp.maximum(m_cur, ch)
    m_prev = m_scr[...]
    m_new = jnp.maximum(m_prev, jnp.max(m_cur, axis=1, keepdims=True))
    alpha = jnp.exp(m_prev - m_new)
    ps = [jnp.exp(ch - m_new) for ch in chunks]
    p_sum = ps[0]
    for p in ps[1:]:
        p_sum = p_sum + p
    l_scr[...] = alpha * l_scr[...] + p_sum
    m_scr[...] = m_new
    pv = _dot(jnp.concatenate(ps, axis=1).astype(BF16), v)
    for c in range(acc_scr.shape[1] // LANES):
        sl = slice(c * LANES, (c + 1) * LANES)
        acc_scr[:, sl] = alpha * acc_scr[:, sl] + pv[:, sl]


def _softmax_init(m_scr, l_scr, acc_scr):
    m_scr[...] = jnp.full(m_scr.shape, NEG_INF, F32)
    l_scr[...] = jnp.zeros(l_scr.shape, F32)
    acc_scr[...] = jnp.zeros(acc_scr.shape, F32)


def _softmax_inv_denominator(l_scr):
    return 1.0 / jnp.sum(l_scr[...], axis=1, keepdims=True)


def _attn_prompt_body(q_ref, k_ref, z_ref, wuv_ref, y_ref, m_scr, l_scr, acc_scr, *, tk, tiles_per_seq):
    nh, tq, dk = q_ref.shape
    rows, lat = acc_scr.shape
    qi = pl.program_id(0) % tiles_per_seq
    q = q_ref[...].reshape(rows, dk)
    _softmax_init(m_scr, l_scr, acc_scr)

    def load_keys(kj):
        return k_ref[pl.ds(pl.multiple_of(kj * tk, tk), tk), :]

    def full_block(kj, carry):
        k = load_keys(kj)
        _softmax_step(_dot_nt(q, k), k[:, :lat], m_scr, l_scr, acc_scr)
        return carry

    n_full = (qi * tq) // tk
    lax.fori_loop(0, n_full, full_block, 0)

    k = load_keys(n_full)
    q_pos = qi * tq + (lax.broadcasted_iota(jnp.int32, (rows, tk), 0) & (tq - 1))
    k_pos = n_full * tk + lax.broadcasted_iota(jnp.int32, (rows, tk), 1)
    s = jnp.where(k_pos <= q_pos, _dot_nt(q, k), NEG_INF)
    _softmax_step(s, k[:, :lat], m_scr, l_scr, acc_scr)

    inv_l = _softmax_inv_denominator(l_scr)
    vd = wuv_ref.shape[2]
    for hd in range(nh):
        ol = (acc_scr[hd * tq:(hd + 1) * tq, :] * inv_l[hd * tq:(hd + 1) * tq, :]).astype(BF16)
        o = _dot(ol, wuv_ref[hd])
        y_ref[:, hd * vd:(hd + 1) * vd] = (o * _silu(z_ref[:, hd * vd:(hd + 1) * vd])).astype(BF16)


def _attn_prompt(qcat, kcat, z, w_uv, *, nseq):
    nh, n_tok, dk = qcat.shape
    lat = w_uv.shape[1]
    t = n_tok // nseq
    tq, tk = 128, 512
    tps = t // tq
    d_attn = z.shape[1]
    return pl.pallas_call(
        functools.partial(_attn_prompt_body, tk=tk, tiles_per_seq=tps),
        grid=(n_tok // tq,),
        in_specs=[
            pl.BlockSpec((nh, tq, dk), lambda i: (0, i, 0)),
            pl.BlockSpec((t, dk), lambda i: (i // tps, 0)),
            pl.BlockSpec((tq, d_attn), lambda i: (i, 0)),
            _resident(w_uv.shape, lambda i: (0, 0, 0)),
        ],
        out_specs=pl.BlockSpec((tq, d_attn), lambda i: (i, 0)),
        out_shape=jax.ShapeDtypeStruct((n_tok, d_attn), BF16),
        scratch_shapes=[pltpu.VMEM((nh * tq, LANES), F32), pltpu.VMEM((nh * tq, LANES), F32),
                        pltpu.VMEM((nh * tq, lat), F32)],
        compiler_params=_params(("arbitrary",), 56),
        name="attn_prompt",
    )(qcat, kcat, z, w_uv)


def _attn_sample_body(pt_ref, q_ref, qp_ref, ckvn_ref, kpen_ref, z_ref, wuv_ref, ck_hbm, kpt_hbm, y_ref,
                      m_scr, l_scr, acc_scr, k_buf, kpt_buf, sem, *, n_pg, n_chunks):
    b = pl.program_id(0)
    page = ck_hbm.shape[1]
    rows, lat = acc_scr.shape
    t_new = ckvn_ref.shape[0]
    n_slots = k_buf.shape[0]
    ahead = n_slots - 1
    assert n_chunks % n_slots == 0 and ahead <= n_chunks

    def chunk_copies(seq, c):
        slot = c % n_slots
        copies = []
        for k in range(n_pg):
            pg = pt_ref[seq, c * n_pg + k]
            copies.append(pltpu.make_async_copy(
                ck_hbm.at[pg], k_buf.at[slot, pl.ds(k * page, page), :], sem.at[0, slot]))
            copies.append(pltpu.make_async_copy(
                kpt_hbm.at[pg], kpt_buf.at[slot, :, pl.ds(k * page, page)], sem.at[1, slot]))
        return copies

    def start_chunk(seq, c):
        for cp in chunk_copies(seq, c):
            cp.start()

    @pl.when(b == 0)
    def _():
        for c in range(ahead):
            start_chunk(b, c)

    _softmax_init(m_scr, l_scr, acc_scr)
    q = q_ref[0].astype(BF16)
    qp = qp_ref[0].astype(BF16)
    for c in range(n_chunks):
        nxt = c + ahead
        if nxt < n_chunks:
            start_chunk(b, nxt)
        else:
            @pl.when(b + 1 < pl.num_programs(0))
            def _():
                start_chunk(b + 1, nxt - n_chunks)
        for cp in chunk_copies(b, c):
            cp.wait()
        kk = k_buf[c % n_slots].astype(BF16)
        kpt = kpt_buf[c % n_slots].astype(BF16)
        _softmax_step(_dot_nt(q, kk) + _dot(qp, kpt), kk, m_scr, l_scr, acc_scr)

    pad = page - t_new
    kn = jnp.concatenate([ckvn_ref[...], jnp.zeros((pad, lat), F32)], axis=0).astype(BF16)
    kpn = jnp.concatenate([kpen_ref[...], jnp.zeros((pad, kpen_ref.shape[1]), F32)], axis=0).astype(BF16)
    s = _dot_nt(q, kn) + _dot_nt(qp, kpn)
    t_q = lax.broadcasted_iota(jnp.int32, (rows, page), 0) & (t_new - 1)
    t_k = lax.broadcasted_iota(jnp.int32, (rows, page), 1)
    _softmax_step(jnp.where(t_k <= t_q, s, NEG_INF), kn, m_scr, l_scr, acc_scr)

    ol = (acc_scr[...] * _softmax_inv_denominator(l_scr)).astype(BF16)
    o_all = _dot(ol, wuv_ref[...])
    vd = wuv_ref.shape[1] // N_HEADS
    for hd in range(N_HEADS):
        o = o_all[hd * t_new:(hd + 1) * t_new, hd * vd:(hd + 1) * vd]
        y_ref[:, hd * vd:(hd + 1) * vd] = o * _silu(z_ref[:, hd * vd:(hd + 1) * vd])


def _attn_sample(qlat, qpe, cache_ckv, cache_kpe_t, page_table, ckv_new, kpe_new, z, w_uv_all):
    nseq, rows, lat = qlat.shape
    rope = qpe.shape[2]
    page = cache_ckv.shape[1]
    n_pages = page_table.shape[1]
    t_new = ckv_new.shape[0] // nseq
    d_attn = z.shape[1]
    n_pg = 16
    n_chunks = n_pages // n_pg
    n_slots = 4

    grid_spec = pltpu.PrefetchScalarGridSpec(
        num_scalar_prefetch=1,
        grid=(nseq,),
        in_specs=[
            pl.BlockSpec((1, rows, lat), lambda b, pt: (b, 0, 0)),
            pl.BlockSpec((1, rows, rope), lambda b, pt: (b, 0, 0)),
            pl.BlockSpec((t_new, lat), lambda b, pt: (b, 0)),
            pl.BlockSpec((t_new, rope), lambda b, pt: (b, 0)),
            pl.BlockSpec((t_new, d_attn), lambda b, pt: (b, 0)),
            _resident(w_uv_all.shape, lambda b, pt: (0, 0)),
            pl.BlockSpec(memory_space=pl.ANY),
            pl.BlockSpec(memory_space=pl.ANY),
        ],
        out_specs=pl.BlockSpec((t_new, d_attn), lambda b, pt: (b, 0)),
        scratch_shapes=[pltpu.VMEM((rows, LANES), F32), pltpu.VMEM((rows, LANES), F32),
                        pltpu.VMEM((rows, lat), F32),
                        pltpu.VMEM((n_slots, n_pg * page, lat), F32),
                        pltpu.VMEM((n_slots, rope, n_pg * page), F32),
                        pltpu.SemaphoreType.DMA((2, n_slots))],
    )
    return pl.pallas_call(
        functools.partial(_attn_sample_body, n_pg=n_pg, n_chunks=n_chunks),
        grid_spec=grid_spec,
        out_shape=jax.ShapeDtypeStruct((nseq * t_new, d_attn), F32),
        compiler_params=_params(("arbitrary",), 48),
        name="attn_sample",
    )(page_table, qlat, qpe, ckv_new, kpe_new, z, w_uv_all, cache_ckv, cache_kpe_t)


def _rope_tables(pos):
    half = QK_ROPE_DIM // 2
    lane = jnp.arange(2 * QK_ROPE_DIM)
    inv_freq = ROPE_THETA ** (-(lane % half).astype(F32) / half)
    ang = pos.astype(F32)[:, None] * inv_freq[None, :]
    sign = jnp.where((lane // half) % 2 == 0, -1.0, 1.0).astype(F32)
    return jnp.cos(ang), jnp.sin(ang) * sign[None, :]


def _swap_halves(w, half):
    return jnp.concatenate([w[..., half:], w[..., :half]], axis=-1)


def kernel(x_prompt, x_sample, state_conv, cache_ckv, cache_kpe, page_table, c_prompt, c_sample, w_ada, b_ada, g_pre, g_post, conv_w_in, conv_w, conv_w_out, kv_g_in, kv_w_down, kv_g_latent, kv_w_up, mla_w_in, mla_g_q, mla_w_uq, mla_w_out):
    nb, seq, d = x_prompt.shape
    nd, t_dec, _ = x_sample.shape
    depth = w_ada.shape[0]
    n_a = conv_w_in.shape[0]
    assert depth == 2 and n_a == 1 and mla_w_in.shape[0] == 1
    lat = kv_g_latent.shape[0]
    rope = QK_ROPE_DIM
    half = rope // 2
    past_len = page_table.shape[1] * cache_ckv.shape[1]

    n_c = nb + nd
    n_c_pad = -(-n_c // 8) * 8
    c_all = jnp.concatenate([c_prompt, c_sample, jnp.zeros((n_c_pad - n_c, d), F32)], axis=0)
    mod = _ada(c_all, w_ada, b_ada)
    mod_p = [mod[l, :nb].reshape(nb, 1, 3 * d) for l in range(depth)]
    mod_s = [mod[l, nb:n_c].reshape(nd, 1, 3 * d) for l in range(depth)]

    w_in0 = conv_w_in[0].astype(BF16)
    w_out0 = conv_w_out[0].astype(BF16)
    y_p, conv_p = _conv_in(x_prompt, mod_p[0], g_pre[0], None, w_in0, conv_w[0], sample=False)
    xp = _out_proj(y_p, x_prompt, mod_p[0], g_post[0], w_out0, name="out_proj_conv_prompt")
    y_s, conv_s = _conv_in(x_sample, mod_s[0], g_pre[0], state_conv[0], w_in0, conv_w[0], sample=True)
    xs = _out_proj(y_s, x_sample, mod_s[0], g_post[0], w_out0, name="out_proj_conv_sample")

    w_kpe = kv_w_down[:, lat:]
    w_kpe_sw = _swap_halves(w_kpe, half)
    w_down_ext = jnp.concatenate([kv_w_down[:, :lat], w_kpe, w_kpe, w_kpe_sw, w_kpe_sw], axis=1).astype(BF16)
    qr = mla_g_q.shape[1]
    w_uq = mla_w_uq[0].reshape(qr, N_HEADS, QK_NOPE_DIM + rope)
    w_uq_pe = w_uq[:, :, QK_NOPE_DIM:]
    w_uq_ext = jnp.concatenate(
        [w_uq[:, :, :QK_NOPE_DIM].reshape(qr, -1), w_uq_pe.reshape(qr, -1),
         _swap_halves(w_uq_pe, half).reshape(qr, -1)], axis=1).astype(BF16)
    w_uk_t = jnp.transpose(kv_w_up[:, :, :QK_NOPE_DIM], (1, 2, 0)).astype(BF16)
    w_uv = jnp.transpose(kv_w_up[:, :, QK_NOPE_DIM:], (1, 0, 2)).astype(BF16)
    w_uv_all = kv_w_up[:, :, QK_NOPE_DIM:].reshape(lat, -1).astype(BF16)
    w_in1 = mla_w_in[0].astype(BF16)
    w_out1 = mla_w_out[0].astype(BF16)

    cos_p, sin_p = _rope_tables(jnp.arange(seq))
    cos_s, sin_s = _rope_tables(past_len + jnp.arange(t_dec))
    reps = 256 // t_dec
    cos_s, sin_s = jnp.tile(cos_s, (reps, 1)), jnp.tile(sin_s, (reps, 1))

    ckv_p, kpe_p, kcat_p, qcat_p, z_p = _mla_in(
        xp, mod_p[1], g_pre[1], kv_g_in, w_down_ext, kv_g_latent, cos_p, sin_p, w_in1, mla_g_q[0], w_uq_ext,
        w_uk_t, sample=False)
    y_p = _attn_prompt(qcat_p, kcat_p, z_p, w_uv, nseq=nb)
    xp = _out_proj(y_p, xp, mod_p[1], g_post[1], w_out1, name="out_proj_mla_prompt")

    ckv_s, kpe_s, qlat_s, qpe_s, z_s = _mla_in(
        xs, mod_s[1], g_pre[1], kv_g_in, w_down_ext, kv_g_latent, cos_s, sin_s, w_in1, mla_g_q[0], w_uq_ext,
        w_uk_t, sample=True)
    cache_kpe_t = jnp.swapaxes(cache_kpe, 1, 2)
    y_s = _attn_sample(qlat_s, qpe_s, cache_ckv, cache_kpe_t, page_table, ckv_s, kpe_s, z_s, w_uv_all)
    xs = _out_proj(y_s, xs, mod_s[1], g_post[1], w_out1, name="out_proj_mla_sample")

    return (xp, xs, conv_p[None], conv_s[None],
            ckv_p.reshape(nb, seq, lat), kpe_p.reshape(nb, seq, rope),
            ckv_s.reshape(nd, t_dec, lat), kpe_s.reshape(nd, t_dec, rope))
```

```python
import functools

import jax
import jax.numpy as jnp
from jax import lax
from jax.experimental import pallas as pl
from jax.experimental.pallas import tpu as pltpu

F32 = jnp.float32
BF16 = jnp.bfloat16
RMS_EPS = 1e-6
ROPE_THETA = 10000.0
MIB = 1024 * 1024
NEG_INF = float("-inf")

N_HEADS = 16
QK_NOPE_DIM = 128
QK_ROPE_DIM = 64
V_HEAD_DIM = 128
SOFTMAX_SCALE = (QK_NOPE_DIM + QK_ROPE_DIM) ** -0.5


def _dot(a, b):
    return jnp.dot(a, b, preferred_element_type=F32)


def _dot_nt(a, b):
    return lax.dot_general(a, b, (((1,), (1,)), ((), ())), preferred_element_type=F32)


def _silu(x):
    return x * jax.nn.sigmoid(x)


def _unit_rms(x):
    return x * lax.rsqrt(jnp.mean(x * x, axis=-1, keepdims=True) + RMS_EPS)


def _params(semantics, vmem_mib):
    return pltpu.CompilerParams(dimension_semantics=semantics, vmem_limit_bytes=vmem_mib * MIB)


def _resident(block_shape, index_map):
    return pl.BlockSpec(block_shape, index_map, pipeline_mode=pl.Buffered(1))


def _ada_body(c_ref, w_ref, b_ref, o_ref):
    a = _silu(c_ref[...]).astype(BF16)
    o_ref[0] = _dot(a, w_ref[0].astype(BF16)) + b_ref[0]


def _ada(c_all, w_ada, b_ada):
    depth, d, d3 = w_ada.shape
    m = c_all.shape[0]
    tn = 1024
    return pl.pallas_call(
        _ada_body,
        grid=(depth, d3 // tn),
        in_specs=[
            pl.BlockSpec((m, d), lambda l, j: (0, 0)),
            pl.BlockSpec((1, d, tn), lambda l, j: (l, 0, j)),
            pl.BlockSpec((1, 1, tn), lambda l, j: (l, 0, j)),
        ],
        out_specs=pl.BlockSpec((1, m, tn), lambda l, j: (l, 0, j)),
        out_shape=jax.ShapeDtypeStruct((depth, m, d3), F32),
        compiler_params=_params(("arbitrary", "arbitrary"), 40),
        name="ada",
    )(c_all, w_ada, b_ada.reshape(depth, 1, d3))


CONV_COL_SPLIT = 2


def _conv_in_body(*refs, sample, tiles_per_seq):
    if sample:
        (x_ref, shift_ref, scale_ref, gpre_ref, st_ref, wb_ref, wc_ref, wv_ref, wz_ref, wconv_ref,
         y_ref, cs_ref, h_scr) = refs
    else:
        (x_ref, shift_ref, scale_ref, gpre_ref, wb_ref, wc_ref, wv_ref, wz_ref, wconv_ref,
         y_ref, cs_ref, h_scr, carry_scr) = refs
    i = pl.program_id(0)
    j = pl.program_id(1)
    g, r, d = x_ref.shape
    rows = g * r

    @pl.when(j == 0)
    def _():
        xn = _unit_rms(x_ref[...])
        h = xn * gpre_ref[...][None] * (1.0 + scale_ref[...]) + shift_ref[...]
        h_scr[...] = h.reshape(rows, d).astype(BF16)

    if not sample:
        @pl.when(i % tiles_per_seq == 0)
        def _():
            carry_scr[j] = jnp.zeros(carry_scr.shape[1:], F32)

    h = h_scr[...]
    tn = y_ref.shape[1]
    tc = tn // CONV_COL_SPLIT
    for cb in range(CONV_COL_SPLIT):
        cols = slice(cb * tc, (cb + 1) * tc)
        bg = _dot(h, wb_ref[:, cols])
        cg = _dot(h, wc_ref[:, cols])
        v = _dot(h, wv_ref[:, cols])
        z = _dot(h, wz_ref[:, cols])
        u = cg * v
        t = lax.broadcasted_iota(jnp.int32, (rows, tc), 0)
        if sample:
            t = t & (r - 1)
            st = st_ref[:, :, cols]
            s0 = jnp.broadcast_to(st[:, 0:1, :], (g, r, tc)).reshape(rows, tc)
            s1 = jnp.broadcast_to(st[:, 1:2, :], (g, r, tc)).reshape(rows, tc)
        else:
            carry = carry_scr[j, :, cols]
            s0 = carry[6:7, :]
            s1 = carry[7:8, :]
        p1 = jnp.where(t == 0, s1, pltpu.roll(u, 1, 0))
        p2 = jnp.where(t == 0, s0, jnp.where(t == 1, s1, pltpu.roll(u, 2, 0)))
        w = wconv_ref[:, cols]
        conv = w[0:1, :] * p2 + w[1:2, :] * p1 + w[2:3, :] * u
        y_ref[:, cols] = (bg * conv * _silu(z)).astype(BF16)
        if sample:
            cs_ref[:, :, cols] = u.reshape(g, r, tc)[:, r - 2:r, :]
        else:
            tail = u[rows - 8:rows, :]
            carry_scr[j, :, cols] = tail
            cs_ref[0, :, cols] = tail


def _conv_in(x, mod, g_pre, state, w_in, w_conv, *, sample):
    nseq, t, d = x.shape
    dc = w_conv.shape[1]
    tn = 512
    nj = dc // tn
    tm = 512
    if sample:
        g, r = tm // t, t
        n_i = nseq // g
        tps = 1
        x_spec = pl.BlockSpec((g, r, d), lambda i, j: (i, 0, 0))
        mod_spec = lambda k: pl.BlockSpec((g, 1, d), lambda i, j: (i, 0, k))
    else:
        g, r = 1, tm
        tps = t // tm
        n_i = nseq * tps
        x_spec = pl.BlockSpec((1, tm, d), lambda i, j: (i // tps, i % tps, 0))
        mod_spec = lambda k: pl.BlockSpec((1, 1, d), lambda i, j: (i // tps, 0, k))
    w_spec = lambda s: pl.BlockSpec((d, tn), lambda i, j: (0, s * nj + j))
    in_specs = [x_spec, mod_spec(0), mod_spec(1), pl.BlockSpec((1, d), lambda i, j: (0, 0))]
    args = [x, mod, mod, g_pre.reshape(1, d)]
    scratch = [pltpu.VMEM((g * r, d), BF16)]
    if sample:
        in_specs.append(pl.BlockSpec((g, 2, tn), lambda i, j: (i, 0, j)))
        args.append(state)
        cs_shape = jax.ShapeDtypeStruct((nseq, 2, dc), F32)
        cs_spec = pl.BlockSpec((g, 2, tn), lambda i, j: (i, 0, j))
    else:
        cs_shape = jax.ShapeDtypeStruct((n_i, 8, dc), F32)
        cs_spec = pl.BlockSpec((1, 8, tn), lambda i, j: (i, 0, j))
        scratch.append(pltpu.VMEM((nj, 8, tn), F32))
    in_specs += [w_spec(0), w_spec(1), w_spec(2), w_spec(3), pl.BlockSpec((3, tn), lambda i, j: (0, j))]
    args += [w_in, w_in, w_in, w_in, w_conv]
    y, cs = pl.pallas_call(
        functools.partial(_conv_in_body, sample=sample, tiles_per_seq=tps),
        grid=(n_i, nj),
        in_specs=in_specs,
        out_specs=[pl.BlockSpec((g * r, tn), lambda i, j: (i, j)), cs_spec],
        out_shape=[jax.ShapeDtypeStruct((nseq * t, dc), BF16), cs_shape],
        scratch_shapes=scratch,
        compiler_params=_params(("arbitrary", "arbitrary"), 56),
        name="conv_in_sample" if sample else "conv_in_prompt",
    )(*args)
    if not sample:
        cs = cs.reshape(nseq, tps, 8, dc)[:, tps - 1, 6:8, :]
    return y, cs


def _out_proj_body(y_ref, x_ref, gate_ref, gpost_ref, w_ref, o_ref):
    g, r, d = x_ref.shape
    o = _dot(y_ref[...].astype(BF16), w_ref[...])
    o = _unit_rms(o) * gpost_ref[...]
    o_ref[...] = x_ref[...] + gate_ref[...] * o.reshape(g, r, d)


def _out_proj(y, x, mod, g_post, w_out, *, name):
    nseq, t, d = x.shape
    tm = 512
    if t >= tm:
        g, r = 1, tm
        tps = t // tm
        x_spec = pl.BlockSpec((1, tm, d), lambda i: (i // tps, i % tps, 0))
        gate_spec = pl.BlockSpec((1, 1, d), lambda i: (i // tps, 0, 2))
    else:
        g, r = tm // t, t
        tps = 1
        x_spec = pl.BlockSpec((g, r, d), lambda i: (i, 0, 0))
        gate_spec = pl.BlockSpec((g, 1, d), lambda i: (i, 0, 2))
    n_i = nseq * t // tm
    return pl.pallas_call(
        _out_proj_body,
        grid=(n_i,),
        in_specs=[
            pl.BlockSpec((tm, y.shape[1]), lambda i: (i, 0)),
            x_spec,
            gate_spec,
            pl.BlockSpec((1, d), lambda i: (0, 0)),
            _resident(w_out.shape, lambda i: (0, 0)),
        ],
        out_specs=x_spec,
        out_shape=jax.ShapeDtypeStruct(x.shape, F32),
        compiler_params=_params(("arbitrary",), 48),
        name=name,
    )(y, x, mod, g_post.reshape(1, d), w_out)


def _mla_in_body(x_ref, shift_ref, scale_ref, gpre_ref, gkv_ref, wdown_ref, glat_ref, cos_ref, sin_ref,
                 win_ref, gq_ref, *refs, sample):
    if sample:
        wuq_ref, wuk_ref, ckv_ref, kpe_ref, z_ref, qlat_ref, qpe_ref = refs
    else:
        cost_ref, sint_ref, wuqt_ref, wuk_ref, wuvt_ref, ckv_ref, kpe_ref, z_ref, kh_ref, vt_ref, qt_ref = refs
    g, r, d = x_ref.shape
    rows = g * r
    lat = glat_ref.shape[1]
    rope = QK_ROPE_DIM
    xn = _unit_rms(x_ref[...])
    cos = cos_ref[...]
    sin = sin_ref[...]

    a = (xn * gkv_ref[...][None]).reshape(rows, d).astype(BF16)
    raw = _dot(a, wdown_ref[...])
    ckv = _unit_rms(raw[:, :lat]) * glat_ref[...]
    kpe2 = raw[:, lat:lat + 128] * cos + raw[:, lat + 128:lat + 256] * sin
    ckv_ref[...] = ckv
    kpe_ref[...] = kpe2[:, :rope]
    nope = QK_NOPE_DIM
    lane = lax.broadcasted_iota(jnp.int32, (rows, 128), 1)
    if not sample:
        ckv_b = ckv.astype(BF16)
        k_nope = _dot(ckv_b, wuk_ref[...])
        v_t = _dot_nt(wuvt_ref[...], ckv_b)
        kpe_pad = jnp.where(lane < rope, kpe2, 0.0).astype(BF16)
        vd = vt_ref.shape[1]
        for hd in range(N_HEADS):
            kh_ref[hd, :, :nope] = k_nope[:, hd * nope:(hd + 1) * nope].astype(BF16)
            kh_ref[hd, :, nope:] = kpe_pad
            vt_ref[hd] = v_t[hd * vd:(hd + 1) * vd, :].astype(BF16)

    h = xn * gpre_ref[...][None] * (1.0 + scale_ref[...]) + shift_ref[...]
    qz = _dot(h.reshape(rows, d).astype(BF16), win_ref[...])
    qr = gq_ref.shape[1]
    z_ref[...] = qz[:, qr:]
    qa = (_unit_rms(qz[:, :qr]) * gq_ref[...]).astype(BF16)
    if not sample:
        q_t = _dot_nt(wuqt_ref[...], qa) * SOFTMAX_SCALE
        cos_t = cost_ref[...]
        sin_t = sint_ref[...]
        hq = nope + rope
        half = rope // 2
        for hd in range(N_HEADS):
            base = hd * hq
            x1 = q_t[base + nope:base + nope + half, :]
            x2 = q_t[base + nope + half:base + hq, :]
            qt_ref[hd, :nope, :] = q_t[base:base + nope, :].astype(BF16)
            qt_ref[hd, nope:nope + half, :] = (x1 * cos_t - x2 * sin_t).astype(BF16)
            qt_ref[hd, nope + half:hq, :] = (x2 * cos_t + x1 * sin_t).astype(BF16)
            qt_ref[hd, hq:, :] = jnp.zeros((qt_ref.shape[1] - hq, rows), BF16)
        return

    q = _dot(qa, wuq_ref[...]) * SOFTMAX_SCALE
    n_nope = N_HEADS * nope
    n_rope = N_HEADS * rope
    for hd in range(N_HEADS):
        qn = q[:, hd * nope:(hd + 1) * nope].astype(BF16)
        ql = _dot(qn, wuk_ref[hd])
        qlat_ref[:, hd * r:(hd + 1) * r, :] = ql.reshape(g, r, lat)
    for pr in range(N_HEADS // 2):
        lo = n_nope + pr * 128
        pe = q[:, lo:lo + 128] * cos + q[:, lo + n_rope:lo + n_rope + 128] * sin
        for k in range(2):
            hd = 2 * pr + k
            qpe_ref[:, hd * r:(hd + 1) * r, :] = pe[:, k * rope:(k + 1) * rope].reshape(g, r, rope)


def _mla_in(x, mod, g_pre, kv_g_in, w_down_ext, kv_g_latent, cos_tab, sin_tab, w_in, g_q, q_operands, *, sample):
    nseq, t, d = x.shape
    n_tok = nseq * t
    lat = kv_g_latent.shape[0]
    rope = QK_ROPE_DIM
    d_attn = w_in.shape[1] - g_q.shape[0]
    tm = 256
    n_i = n_tok // tm
    row_spec = lambda w: pl.BlockSpec((tm, w), lambda i: (i, 0))
    if sample:
        g, r = tm // t, t
        x_spec = pl.BlockSpec((g, r, d), lambda i: (i, 0, 0))
        mod_spec = lambda k: pl.BlockSpec((g, 1, d), lambda i: (i, 0, k))
        tab_spec = pl.BlockSpec((tm, 128), lambda i: (0, 0))
        qk_shapes = [jax.ShapeDtypeStruct((nseq, N_HEADS * t, lat), F32),
                     jax.ShapeDtypeStruct((nseq, N_HEADS * t, rope), F32)]
        qk_specs = [pl.BlockSpec((g, N_HEADS * t, lat), lambda i: (i, 0, 0)),
                    pl.BlockSpec((g, N_HEADS * t, rope), lambda i: (i, 0, 0))]
        w_uq_ext, w_uk_t = q_operands
        q_in_specs = [_resident(w_uq_ext.shape, lambda i: (0, 0)), _resident(w_uk_t.shape, lambda i: (0, 0, 0))]
    else:
        tps = t // tm
        x_spec = pl.BlockSpec((1, tm, d), lambda i: (i // tps, i % tps, 0))
        mod_spec = lambda k: pl.BlockSpec((1, 1, d), lambda i: (i // tps, 0, k))
        tab_spec = pl.BlockSpec((tm, 128), lambda i: (i % tps, 0))
        hk = QK_NOPE_DIM + 128
        qk_shapes = [jax.ShapeDtypeStruct((N_HEADS, n_tok, hk), BF16),
                     jax.ShapeDtypeStruct((N_HEADS, V_HEAD_DIM, n_tok), BF16),
                     jax.ShapeDtypeStruct((N_HEADS, hk, n_tok), BF16)]
        qk_specs = [pl.BlockSpec((N_HEADS, tm, hk), lambda i: (0, i, 0)),
                    pl.BlockSpec((N_HEADS, V_HEAD_DIM, tm), lambda i: (0, 0, i)),
                    pl.BlockSpec((N_HEADS, hk, tm), lambda i: (0, 0, i))]
        cos_t, sin_t, w_uq_t, w_uk_all, w_uv_t_all = q_operands
        tab_t_spec = pl.BlockSpec((cos_t.shape[0], tm), lambda i: (0, i % tps))
        q_in_specs = [tab_t_spec, tab_t_spec, _resident(w_uq_t.shape, lambda i: (0, 0)),
                      _resident(w_uk_all.shape, lambda i: (0, 0)), _resident(w_uv_t_all.shape, lambda i: (0, 0))]
    vec = lambda n: pl.BlockSpec((1, n), lambda i: (0, 0))
    return pl.pallas_call(
        functools.partial(_mla_in_body, sample=sample),
        grid=(n_i,),
        in_specs=[
            x_spec, mod_spec(0), mod_spec(1), vec(d), vec(d),
            _resident(w_down_ext.shape, lambda i: (0, 0)),
            vec(lat), tab_spec, tab_spec,
            _resident(w_in.shape, lambda i: (0, 0)),
            vec(g_q.shape[0]),
        ] + q_in_specs,
        out_specs=[row_spec(lat), row_spec(rope), row_spec(d_attn)] + qk_specs,
        out_shape=[jax.ShapeDtypeStruct((n_tok, lat), F32), jax.ShapeDtypeStruct((n_tok, rope), F32),
                   jax.ShapeDtypeStruct((n_tok, d_attn), F32)] + qk_shapes,
        compiler_params=_params(("arbitrary",), 58),
        name="mla_in_sample" if sample else "mla_in_prompt",
    )(x, mod, mod, g_pre.reshape(1, d), kv_g_in.reshape(1, d), w_down_ext, kv_g_latent.reshape(1, lat),
      cos_tab, sin_tab, w_in, g_q.reshape(1, -1), *q_operands)


LANES = 128


def _softmax_step(s, v, m_scr, l_scr, acc_scr):
    n_chunks = s.shape[1] // LANES
    chunks = [s[:, c * LANES:(c + 1) * LANES] for c in range(n_chunks)]
    m_cur = chunks[0]
    for ch in chunks[1:]:
        m_cur = jnp.maximum(m_cur, ch)
    m_prev = m_scr[...]
    m_new = jnp.maximum(m_prev, jnp.max(m_cur, axis=1, keepdims=True))
    alpha = jnp.exp(m_prev - m_new)
    ps = [jnp.exp(ch - m_new) for ch in chunks]
    p_sum = ps[0]
    for p in ps[1:]:
        p_sum = p_sum + p
    l_scr[...] = alpha * l_scr[...] + p_sum
    m_scr[...] = m_new
    pv = _dot(jnp.concatenate(ps, axis=1).astype(BF16), v)
    for c in range(acc_scr.shape[1] // LANES):
        sl = slice(c * LANES, (c + 1) * LANES)
        acc_scr[:, sl] = alpha * acc_scr[:, sl] + pv[:, sl]


def _softmax_init(m_scr, l_scr, acc_scr):
    m_scr[...] = jnp.full(m_scr.shape, NEG_INF, F32)
    l_scr[...] = jnp.zeros(l_scr.shape, F32)
    acc_scr[...] = jnp.zeros(acc_scr.shape, F32)


def _softmax_inv_denominator(l_scr):
    return 1.0 / jnp.sum(l_scr[...], axis=1, keepdims=True)


def _attn_prompt_body(qt_ref, k_ref, vt_ref, z_ref, y_ref, m_scr, l_scr, acc_scr, *, tq):
    t_seq = k_ref.shape[1]
    row = lax.broadcasted_iota(jnp.int32, (tq, tq), 0)
    col = lax.broadcasted_iota(jnp.int32, (tq, tq), 1)
    on_or_below_diagonal = row <= col
    for qi in range(t_seq // tq):
        qs = slice(qi * tq, (qi + 1) * tq)
        q_t = qt_ref[0, :, qs]
        for kj in range(qi + 1):
            ks = slice(kj * tq, (kj + 1) * tq)
            s_t = _dot(k_ref[0, ks, :], q_t)
            if kj == qi:
                s_t = jnp.where(on_or_below_diagonal, s_t, NEG_INF)
            m_cur = jnp.max(s_t, axis=0, keepdims=True)
            if kj == 0:
                m_new = m_cur
            else:
                m_prev = m_scr[...]
                m_new = jnp.maximum(m_prev, m_cur)
                alpha = jnp.exp(m_prev - m_new)
            p_t = jnp.exp(s_t - m_new)
            p_sum = jnp.sum(p_t, axis=0, keepdims=True)
            pv_t = _dot(vt_ref[0, :, ks], p_t.astype(BF16))
            if kj == 0:
                l_scr[...] = p_sum
                acc_scr[...] = pv_t
            else:
                l_scr[...] = alpha * l_scr[...] + p_sum
                acc_scr[...] = alpha * acc_scr[...] + pv_t
            m_scr[...] = m_new
        o = (acc_scr[...] * (1.0 / l_scr[...])).T
        y_ref[qs, :] = (o * _silu(z_ref[qs, :])).astype(BF16)


def _attn_prompt(q_t, k_heads, v_t, z, *, nseq):
    nh, hk, n_tok = q_t.shape
    vd = v_t.shape[1]
    t = n_tok // nseq
    tq = 512
    return pl.pallas_call(
        functools.partial(_attn_prompt_body, tq=tq),
        grid=(nseq, nh),
        in_specs=[
            pl.BlockSpec((1, hk, t), lambda b, h: (h, 0, b)),
            pl.BlockSpec((1, t, hk), lambda b, h: (h, b, 0)),
            pl.BlockSpec((1, vd, t), lambda b, h: (h, 0, b)),
            pl.BlockSpec((t, vd), lambda b, h: (b, h)),
        ],
        out_specs=pl.BlockSpec((t, vd), lambda b, h: (b, h)),
        out_shape=jax.ShapeDtypeStruct((n_tok, nh * vd), BF16),
        scratch_shapes=[pltpu.VMEM((1, tq), F32), pltpu.VMEM((1, tq), F32), pltpu.VMEM((vd, tq), F32)],
        compiler_params=_params(("arbitrary", "arbitrary"), 48),
        name="attn_prompt",
    )(q_t, k_heads, v_t, z)


def _attn_sample_body(pt_ref, q_ref, qp_ref, ckvn_ref, kpen_ref, z_ref, wuv_ref, ck_hbm, kpt_hbm, y_ref,
                      m_scr, l_scr, acc_scr, k_buf, kpt_buf, sem, *, n_pg, n_chunks):
    b = pl.program_id(0)
    page = ck_hbm.shape[1]
    rows, lat = acc_scr.shape
    t_new = ckvn_ref.shape[0]
    n_slots = k_buf.shape[0]
    ahead = n_slots - 1
    assert n_chunks % n_slots == 0 and ahead <= n_chunks

    def chunk_copies(seq, c):
        slot = c % n_slots
        copies = []
        for k in range(n_pg):
            pg = pt_ref[seq, c * n_pg + k]
            copies.append(pltpu.make_async_copy(
                ck_hbm.at[pg], k_buf.at[slot, pl.ds(k * page, page), :], sem.at[0, slot]))
            copies.append(pltpu.make_async_copy(
                kpt_hbm.at[pg], kpt_buf.at[slot, :, pl.ds(k * page, page)], sem.at[1, slot]))
        return copies

    def start_chunk(seq, c):
        for cp in chunk_copies(seq, c):
            cp.start()

    @pl.when(b == 0)
    def _():
        for c in range(ahead):
            start_chunk(b, c)

    _softmax_init(m_scr, l_scr, acc_scr)
    q = q_ref[0].astype(BF16)
    qp = qp_ref[0].astype(BF16)
    for c in range(n_chunks):
        nxt = c + ahead
        if nxt < n_chunks:
            start_chunk(b, nxt)
        else:
            @pl.when(b + 1 < pl.num_programs(0))
            def _():
                start_chunk(b + 1, nxt - n_chunks)
        for cp in chunk_copies(b, c):
            cp.wait()
        kk = k_buf[c % n_slots].astype(BF16)
        kpt = kpt_buf[c % n_slots].astype(BF16)
        _softmax_step(_dot_nt(q, kk) + _dot(qp, kpt), kk, m_scr, l_scr, acc_scr)

    pad = page - t_new
    kn = jnp.concatenate([ckvn_ref[...], jnp.zeros((pad, lat), F32)], axis=0).astype(BF16)
    kpn = jnp.concatenate([kpen_ref[...], jnp.zeros((pad, kpen_ref.shape[1]), F32)], axis=0).astype(BF16)
    s = _dot_nt(q, kn) + _dot_nt(qp, kpn)
    t_q = lax.broadcasted_iota(jnp.int32, (rows, page), 0) & (t_new - 1)
    t_k = lax.broadcasted_iota(jnp.int32, (rows, page), 1)
    _softmax_step(jnp.where(t_k <= t_q, s, NEG_INF), kn, m_scr, l_scr, acc_scr)

    ol = (acc_scr[...] * _softmax_inv_denominator(l_scr)).astype(BF16)
    o_all = _dot(ol, wuv_ref[...])
    vd = wuv_ref.shape[1] // N_HEADS
    for hd in range(N_HEADS):
        o = o_all[hd * t_new:(hd + 1) * t_new, hd * vd:(hd + 1) * vd]
        y_ref[:, hd * vd:(hd + 1) * vd] = o * _silu(z_ref[:, hd * vd:(hd + 1) * vd])


def _attn_sample(qlat, qpe, cache_ckv, cache_kpe_t, page_table, ckv_new, kpe_new, z, w_uv_all):
    nseq, rows, lat = qlat.shape
    rope = qpe.shape[2]
    page = cache_ckv.shape[1]
    n_pages = page_table.shape[1]
    t_new = ckv_new.shape[0] // nseq
    d_attn = z.shape[1]
    n_pg = 16
    n_chunks = n_pages // n_pg
    n_slots = 4

    grid_spec = pltpu.PrefetchScalarGridSpec(
        num_scalar_prefetch=1,
        grid=(nseq,),
        in_specs=[
            pl.BlockSpec((1, rows, lat), lambda b, pt: (b, 0, 0)),
            pl.BlockSpec((1, rows, rope), lambda b, pt: (b, 0, 0)),
            pl.BlockSpec((t_new, lat), lambda b, pt: (b, 0)),
            pl.BlockSpec((t_new, rope), lambda b, pt: (b, 0)),
            pl.BlockSpec((t_new, d_attn), lambda b, pt: (b, 0)),
            _resident(w_uv_all.shape, lambda b, pt: (0, 0)),
            pl.BlockSpec(memory_space=pl.ANY),
            pl.BlockSpec(memory_space=pl.ANY),
        ],
        out_specs=pl.BlockSpec((t_new, d_attn), lambda b, pt: (b, 0)),
        scratch_shapes=[pltpu.VMEM((rows, LANES), F32), pltpu.VMEM((rows, LANES), F32),
                        pltpu.VMEM((rows, lat), F32),
                        pltpu.VMEM((n_slots, n_pg * page, lat), F32),
                        pltpu.VMEM((n_slots, rope, n_pg * page), F32),
                        pltpu.SemaphoreType.DMA((2, n_slots))],
    )
    return pl.pallas_call(
        functools.partial(_attn_sample_body, n_pg=n_pg, n_chunks=n_chunks),
        grid_spec=grid_spec,
        out_shape=jax.ShapeDtypeStruct((nseq * t_new, d_attn), F32),
        compiler_params=_params(("arbitrary",), 48),
        name="attn_sample",
    )(page_table, qlat, qpe, ckv_new, kpe_new, z, w_uv_all, cache_ckv, cache_kpe_t)


def _rope_tables(pos):
    half = QK_ROPE_DIM // 2
    lane = jnp.arange(2 * QK_ROPE_DIM)
    inv_freq = ROPE_THETA ** (-(lane % half).astype(F32) / half)
    ang = pos.astype(F32)[:, None] * inv_freq[None, :]
    sign = jnp.where((lane // half) % 2 == 0, -1.0, 1.0).astype(F32)
    return jnp.cos(ang), jnp.sin(ang) * sign[None, :]


def _rope_tables_transposed(pos):
    half = QK_ROPE_DIM // 2
    inv_freq = ROPE_THETA ** (-jnp.arange(half, dtype=F32) / half)
    ang = inv_freq[:, None] * pos.astype(F32)[None, :]
    return jnp.cos(ang), jnp.sin(ang)


def _swap_halves(w, half):
    return jnp.concatenate([w[..., half:], w[..., :half]], axis=-1)


def kernel(x_prompt, x_sample, state_conv, cache_ckv, cache_kpe, page_table, c_prompt, c_sample, w_ada, b_ada, g_pre, g_post, conv_w_in, conv_w, conv_w_out, kv_g_in, kv_w_down, kv_g_latent, kv_w_up, mla_w_in, mla_g_q, mla_w_uq, mla_w_out):
    nb, seq, d = x_prompt.shape
    nd, t_dec, _ = x_sample.shape
    depth = w_ada.shape[0]
    n_a = conv_w_in.shape[0]
    assert depth == 2 and n_a == 1 and mla_w_in.shape[0] == 1
    lat = kv_g_latent.shape[0]
    rope = QK_ROPE_DIM
    half = rope // 2
    past_len = page_table.shape[1] * cache_ckv.shape[1]

    n_c = nb + nd
    n_c_pad = -(-n_c // 8) * 8
    c_all = jnp.concatenate([c_prompt, c_sample, jnp.zeros((n_c_pad - n_c, d), F32)], axis=0)
    mod = _ada(c_all, w_ada, b_ada)
    mod_p = [mod[l, :nb].reshape(nb, 1, 3 * d) for l in range(depth)]
    mod_s = [mod[l, nb:n_c].reshape(nd, 1, 3 * d) for l in range(depth)]

    w_in0 = conv_w_in[0].astype(BF16)
    w_out0 = conv_w_out[0].astype(BF16)
    y_p, conv_p = _conv_in(x_prompt, mod_p[0], g_pre[0], None, w_in0, conv_w[0], sample=False)
    xp = _out_proj(y_p, x_prompt, mod_p[0], g_post[0], w_out0, name="out_proj_conv_prompt")
    y_s, conv_s = _conv_in(x_sample, mod_s[0], g_pre[0], state_conv[0], w_in0, conv_w[0], sample=True)
    xs = _out_proj(y_s, x_sample, mod_s[0], g_post[0], w_out0, name="out_proj_conv_sample")

    w_kpe = kv_w_down[:, lat:]
    w_kpe_sw = _swap_halves(w_kpe, half)
    w_down_ext = jnp.concatenate([kv_w_down[:, :lat], w_kpe, w_kpe, w_kpe_sw, w_kpe_sw], axis=1).astype(BF16)
    qr = mla_g_q.shape[1]
    w_uq = mla_w_uq[0].reshape(qr, N_HEADS, QK_NOPE_DIM + rope)
    w_uq_pe = w_uq[:, :, QK_NOPE_DIM:]
    w_uq_ext = jnp.concatenate(
        [w_uq[:, :, :QK_NOPE_DIM].reshape(qr, -1), w_uq_pe.reshape(qr, -1),
         _swap_halves(w_uq_pe, half).reshape(qr, -1)], axis=1).astype(BF16)
    w_uk_t = jnp.transpose(kv_w_up[:, :, :QK_NOPE_DIM], (1, 2, 0)).astype(BF16)
    w_uk_all = kv_w_up[:, :, :QK_NOPE_DIM].reshape(lat, -1).astype(BF16)
    w_uv_all = kv_w_up[:, :, QK_NOPE_DIM:].reshape(lat, -1).astype(BF16)
    w_uv_t_all = jnp.transpose(kv_w_up[:, :, QK_NOPE_DIM:], (1, 2, 0)).reshape(-1, lat).astype(BF16)
    w_uq_t = mla_w_uq[0].T.astype(BF16)
    w_in1 = mla_w_in[0].astype(BF16)
    w_out1 = mla_w_out[0].astype(BF16)

    cos_p, sin_p = _rope_tables(jnp.arange(seq))
    cos_s, sin_s = _rope_tables(past_len + jnp.arange(t_dec))
    reps = 256 // t_dec
    cos_s, sin_s = jnp.tile(cos_s, (reps, 1)), jnp.tile(sin_s, (reps, 1))
    cos_t, sin_t = _rope_tables_transposed(jnp.arange(seq))

    ckv_p, kpe_p, z_p, k_heads, v_t, q_t = _mla_in(
        xp, mod_p[1], g_pre[1], kv_g_in, w_down_ext, kv_g_latent, cos_p, sin_p, w_in1, mla_g_q[0],
        (cos_t, sin_t, w_uq_t, w_uk_all, w_uv_t_all), sample=False)
    y_p = _attn_prompt(q_t, k_heads, v_t, z_p, nseq=nb)
    xp = _out_proj(y_p, xp, mod_p[1], g_post[1], w_out1, name="out_proj_mla_prompt")

    ckv_s, kpe_s, z_s, qlat_s, qpe_s = _mla_in(
        xs, mod_s[1], g_pre[1], kv_g_in, w_down_ext, kv_g_latent, cos_s, sin_s, w_in1, mla_g_q[0],
        (w_uq_ext, w_uk_t), sample=True)
    cache_kpe_t = jnp.swapaxes(cache_kpe, 1, 2)
    y_s = _attn_sample(qlat_s, qpe_s, cache_ckv, cache_kpe_t, page_table, ckv_s, kpe_s, z_s, w_uv_all)
    xs = _out_proj(y_s, xs, mod_s[1], g_post[1], w_out1, name="out_proj_mla_sample")

    return (xp, xs, conv_p[None], conv_s[None],
            ckv_p.reshape(nb, seq, lat), kpe_p.reshape(nb, seq, rope),
            ckv_s.reshape(nd, t_dec, lat), kpe_s.reshape(nd, t_dec, rope))
```

```python
import functools

import jax
import jax.numpy as jnp
from jax import lax
from jax.experimental import pallas as pl
from jax.experimental.pallas import tpu as pltpu

F32 = jnp.float32
BF16 = jnp.bfloat16
RMS_EPS = 1e-6
ROPE_THETA = 10000.0
MIB = 1024 * 1024
NEG_INF = float("-inf")

N_HEADS = 16
QK_NOPE_DIM = 128
QK_ROPE_DIM = 64
V_HEAD_DIM = 128
SOFTMAX_SCALE = (QK_NOPE_DIM + QK_ROPE_DIM) ** -0.5


def _dot(a, b):
    return jnp.dot(a, b, preferred_element_type=F32)


def _dot_nt(a, b):
    return lax.dot_general(a, b, (((1,), (1,)), ((), ())), preferred_element_type=F32)


def _silu(x):
    return x * jax.nn.sigmoid(x)


def _unit_rms(x):
    return x * lax.rsqrt(jnp.mean(x * x, axis=-1, keepdims=True) + RMS_EPS)


def _params(semantics, vmem_mib):
    return pltpu.CompilerParams(dimension_semantics=semantics, vmem_limit_bytes=vmem_mib * MIB)


def _resident(block_shape, index_map):
    return pl.BlockSpec(block_shape, index_map, pipeline_mode=pl.Buffered(1))


def _ada_body(c_ref, w_ref, b_ref, o_ref):
    a = _silu(c_ref[...]).astype(BF16)
    o_ref[0] = _dot(a, w_ref[0].astype(BF16)) + b_ref[0]


def _ada(c_all, w_ada, b_ada):
    depth, d, d3 = w_ada.shape
    m = c_all.shape[0]
    tn = 1024
    return pl.pallas_call(
        _ada_body,
        grid=(depth, d3 // tn),
        in_specs=[
            pl.BlockSpec((m, d), lambda l, j: (0, 0)),
            pl.BlockSpec((1, d, tn), lambda l, j: (l, 0, j)),
            pl.BlockSpec((1, 1, tn), lambda l, j: (l, 0, j)),
        ],
        out_specs=pl.BlockSpec((1, m, tn), lambda l, j: (l, 0, j)),
        out_shape=jax.ShapeDtypeStruct((depth, m, d3), F32),
        compiler_params=_params(("arbitrary", "arbitrary"), 40),
        name="ada",
    )(c_all, w_ada, b_ada.reshape(depth, 1, d3))


CONV_COL_SPLIT = 2


def _conv_in_body(*refs, sample, tiles_per_seq):
    if sample:
        (x_ref, shift_ref, scale_ref, gpre_ref, st_ref, wb_ref, wc_ref, wv_ref, wz_ref, wconv_ref,
         y_ref, cs_ref, h_scr) = refs
    else:
        (x_ref, shift_ref, scale_ref, gpre_ref, wb_ref, wc_ref, wv_ref, wz_ref, wconv_ref,
         y_ref, cs_ref, h_scr, carry_scr) = refs
    i = pl.program_id(0)
    j = pl.program_id(1)
    g, r, d = x_ref.shape
    rows = g * r

    @pl.when(j == 0)
    def _():
        xn = _unit_rms(x_ref[...])
        h = xn * gpre_ref[...][None] * (1.0 + scale_ref[...]) + shift_ref[...]
        h_scr[...] = h.reshape(rows, d).astype(BF16)

    if not sample:
        @pl.when(i % tiles_per_seq == 0)
        def _():
            carry_scr[j] = jnp.zeros(carry_scr.shape[1:], F32)

    h = h_scr[...]
    tn = y_ref.shape[1]
    tc = tn // CONV_COL_SPLIT
    for cb in range(CONV_COL_SPLIT):
        cols = slice(cb * tc, (cb + 1) * tc)
        bg = _dot(h, wb_ref[:, cols])
        cg = _dot(h, wc_ref[:, cols])
        v = _dot(h, wv_ref[:, cols])
        z = _dot(h, wz_ref[:, cols])
        u = cg * v
        t = lax.broadcasted_iota(jnp.int32, (rows, tc), 0)
        if sample:
            t = t & (r - 1)
            st = st_ref[:, :, cols]
            s0 = jnp.broadcast_to(st[:, 0:1, :], (g, r, tc)).reshape(rows, tc)
            s1 = jnp.broadcast_to(st[:, 1:2, :], (g, r, tc)).reshape(rows, tc)
        else:
            carry = carry_scr[j, :, cols]
            s0 = carry[6:7, :]
            s1 = carry[7:8, :]
        p1 = jnp.where(t == 0, s1, pltpu.roll(u, 1, 0))
        p2 = jnp.where(t == 0, s0, jnp.where(t == 1, s1, pltpu.roll(u, 2, 0)))
        w = wconv_ref[:, cols]
        conv = w[0:1, :] * p2 + w[1:2, :] * p1 + w[2:3, :] * u
        y_ref[:, cols] = (bg * conv * _silu(z)).astype(BF16)
        if sample:
            cs_ref[:, :, cols] = u.reshape(g, r, tc)[:, r - 2:r, :]
        else:
            tail = u[rows - 8:rows, :]
            carry_scr[j, :, cols] = tail
            cs_ref[0, :, cols] = tail


def _conv_in(x, mod, g_pre, state, w_in, w_conv, *, sample):
    nseq, t, d = x.shape
    dc = w_conv.shape[1]
    tn = 512
    nj = dc // tn
    tm = 512
    if sample:
        g, r = tm // t, t
        n_i = nseq // g
        tps = 1
        x_spec = pl.BlockSpec((g, r, d), lambda i, j: (i, 0, 0))
        mod_spec = lambda k: pl.BlockSpec((g, 1, d), lambda i, j: (i, 0, k))
    else:
        g, r = 1, tm
        tps = t // tm
        n_i = nseq * tps
        x_spec = pl.BlockSpec((1, tm, d), lambda i, j: (i // tps, i % tps, 0))
        mod_spec = lambda k: pl.BlockSpec((1, 1, d), lambda i, j: (i // tps, 0, k))
    w_spec = lambda s: pl.BlockSpec((d, tn), lambda i, j: (0, s * nj + j))
    in_specs = [x_spec, mod_spec(0), mod_spec(1), pl.BlockSpec((1, d), lambda i, j: (0, 0))]
    args = [x, mod, mod, g_pre.reshape(1, d)]
    scratch = [pltpu.VMEM((g * r, d), BF16)]
    if sample:
        in_specs.append(pl.BlockSpec((g, 2, tn), lambda i, j: (i, 0, j)))
        args.append(state)
        cs_shape = jax.ShapeDtypeStruct((nseq, 2, dc), F32)
        cs_spec = pl.BlockSpec((g, 2, tn), lambda i, j: (i, 0, j))
    else:
        cs_shape = jax.ShapeDtypeStruct((n_i, 8, dc), F32)
        cs_spec = pl.BlockSpec((1, 8, tn), lambda i, j: (i, 0, j))
        scratch.append(pltpu.VMEM((nj, 8, tn), F32))
    in_specs += [w_spec(0), w_spec(1), w_spec(2), w_spec(3), pl.BlockSpec((3, tn), lambda i, j: (0, j))]
    args += [w_in, w_in, w_in, w_in, w_conv]
    y, cs = pl.pallas_call(
        functools.partial(_conv_in_body, sample=sample, tiles_per_seq=tps),
        grid=(n_i, nj),
        in_specs=in_specs,
        out_specs=[pl.BlockSpec((g * r, tn), lambda i, j: (i, j)), cs_spec],
        out_shape=[jax.ShapeDtypeStruct((nseq * t, dc), BF16), cs_shape],
        scratch_shapes=scratch,
        compiler_params=_params(("arbitrary", "arbitrary"), 56),
        name="conv_in_sample" if sample else "conv_in_prompt",
    )(*args)
    if not sample:
        cs = cs.reshape(nseq, tps, 8, dc)[:, tps - 1, 6:8, :]
    return y, cs


def _out_proj_body(y_ref, x_ref, gate_ref, gpost_ref, w_ref, o_ref):
    g, r, d = x_ref.shape
    o = _dot(y_ref[...].astype(BF16), w_ref[...])
    o = _unit_rms(o) * gpost_ref[...]
    o_ref[...] = x_ref[...] + gate_ref[...] * o.reshape(g, r, d)


def _out_proj(y, x, mod, g_post, w_out, *, name):
    nseq, t, d = x.shape
    tm = 512
    if t >= tm:
        g, r = 1, tm
        tps = t // tm
        x_spec = pl.BlockSpec((1, tm, d), lambda i: (i // tps, i % tps, 0))
        gate_spec = pl.BlockSpec((1, 1, d), lambda i: (i // tps, 0, 2))
    else:
        g, r = tm // t, t
        tps = 1
        x_spec = pl.BlockSpec((g, r, d), lambda i: (i, 0, 0))
        gate_spec = pl.BlockSpec((g, 1, d), lambda i: (i, 0, 2))
    n_i = nseq * t // tm
    return pl.pallas_call(
        _out_proj_body,
        grid=(n_i,),
        in_specs=[
            pl.BlockSpec((tm, y.shape[1]), lambda i: (i, 0)),
            x_spec,
            gate_spec,
            pl.BlockSpec((1, d), lambda i: (0, 0)),
            _resident(w_out.shape, lambda i: (0, 0)),
        ],
        out_specs=x_spec,
        out_shape=jax.ShapeDtypeStruct(x.shape, F32),
        compiler_params=_params(("arbitrary",), 48),
        name=name,
    )(y, x, mod, g_post.reshape(1, d), w_out)


def _mla_in_body(x_ref, shift_ref, scale_ref, gpre_ref, gkv_ref, wdown_ref, glat_ref, cos_ref, sin_ref,
                 win_ref, gq_ref, *refs, sample):
    if sample:
        wuq_ref, wuk_ref, ckv_ref, kpe_ref, z_ref, qlat_ref, qpe_ref = refs
    else:
        cost_ref, sint_ref, wuqt_ref, wuk_ref, wuvt_ref, ckv_ref, kpe_ref, z_ref, kh_ref, vt_ref, qt_ref = refs
    g, r, d = x_ref.shape
    rows = g * r
    lat = glat_ref.shape[1]
    rope = QK_ROPE_DIM
    xn = _unit_rms(x_ref[...])
    cos = cos_ref[...]
    sin = sin_ref[...]

    a = (xn * gkv_ref[...][None]).reshape(rows, d).astype(BF16)
    raw = _dot(a, wdown_ref[...])
    ckv = _unit_rms(raw[:, :lat]) * glat_ref[...]
    kpe2 = raw[:, lat:lat + 128] * cos + raw[:, lat + 128:lat + 256] * sin
    ckv_ref[...] = ckv
    kpe_ref[...] = kpe2[:, :rope]
    nope = QK_NOPE_DIM
    lane = lax.broadcasted_iota(jnp.int32, (rows, 128), 1)
    if not sample:
        ckv_b = ckv.astype(BF16)
        k_nope = _dot(ckv_b, wuk_ref[...])
        v_t = _dot_nt(wuvt_ref[...], ckv_b)
        kpe_pad = jnp.where(lane < rope, kpe2, 0.0).astype(BF16)
        vd = vt_ref.shape[1]
        for hd in range(N_HEADS):
            kh_ref[hd, :, :nope] = k_nope[:, hd * nope:(hd + 1) * nope].astype(BF16)
            kh_ref[hd, :, nope:] = kpe_pad
            vt_ref[hd] = v_t[hd * vd:(hd + 1) * vd, :].astype(BF16)

    h = xn * gpre_ref[...][None] * (1.0 + scale_ref[...]) + shift_ref[...]
    qz = _dot(h.reshape(rows, d).astype(BF16), win_ref[...])
    qr = gq_ref.shape[1]
    z_ref[...] = qz[:, qr:]
    qa = (_unit_rms(qz[:, :qr]) * gq_ref[...]).astype(BF16)
    if not sample:
        q_t = _dot_nt(wuqt_ref[...], qa) * SOFTMAX_SCALE
        cos_t = cost_ref[...]
        sin_t = sint_ref[...]
        hq = nope + rope
        half = rope // 2
        for hd in range(N_HEADS):
            base = hd * hq
            x1 = q_t[base + nope:base + nope + half, :]
            x2 = q_t[base + nope + half:base + hq, :]
            qt_ref[hd, :nope, :] = q_t[base:base + nope, :].astype(BF16)
            qt_ref[hd, nope:nope + half, :] = (x1 * cos_t - x2 * sin_t).astype(BF16)
            qt_ref[hd, nope + half:hq, :] = (x2 * cos_t + x1 * sin_t).astype(BF16)
            qt_ref[hd, hq:, :] = jnp.zeros((qt_ref.shape[1] - hq, rows), BF16)
        return

    q = _dot(qa, wuq_ref[...]) * SOFTMAX_SCALE
    n_nope = N_HEADS * nope
    n_rope = N_HEADS * rope
    for hd in range(N_HEADS):
        qn = q[:, hd * nope:(hd + 1) * nope].astype(BF16)
        ql = _dot(qn, wuk_ref[hd])
        qlat_ref[:, hd * r:(hd + 1) * r, :] = ql.reshape(g, r, lat)
    for pr in range(N_HEADS // 2):
        lo = n_nope + pr * 128
        pe = q[:, lo:lo + 128] * cos + q[:, lo + n_rope:lo + n_rope + 128] * sin
        for k in range(2):
            hd = 2 * pr + k
            qpe_ref[:, hd * r:(hd + 1) * r, :] = pe[:, k * rope:(k + 1) * rope].reshape(g, r, rope)


def _mla_in(x, mod, g_pre, kv_g_in, w_down_ext, kv_g_latent, cos_tab, sin_tab, w_in, g_q, q_operands, *, sample):
    nseq, t, d = x.shape
    n_tok = nseq * t
    lat = kv_g_latent.shape[0]
    rope = QK_ROPE_DIM
    d_attn = w_in.shape[1] - g_q.shape[0]
    tm = 256
    n_i = n_tok // tm
    row_spec = lambda w: pl.BlockSpec((tm, w), lambda i: (i, 0))
    if sample:
        g, r = tm // t, t
        x_spec = pl.BlockSpec((g, r, d), lambda i: (i, 0, 0))
        mod_spec = lambda k: pl.BlockSpec((g, 1, d), lambda i: (i, 0, k))
        tab_spec = pl.BlockSpec((tm, 128), lambda i: (0, 0))
        qk_shapes = [jax.ShapeDtypeStruct((nseq, N_HEADS * t, lat), F32),
                     jax.ShapeDtypeStruct((nseq, N_HEADS * t, rope), F32)]
        qk_specs = [pl.BlockSpec((g, N_HEADS * t, lat), lambda i: (i, 0, 0)),
                    pl.BlockSpec((g, N_HEADS * t, rope), lambda i: (i, 0, 0))]
        w_uq_ext, w_uk_t = q_operands
        q_in_specs = [_resident(w_uq_ext.shape, lambda i: (0, 0)), _resident(w_uk_t.shape, lambda i: (0, 0, 0))]
    else:
        tps = t // tm
        x_spec = pl.BlockSpec((1, tm, d), lambda i: (i // tps, i % tps, 0))
        mod_spec = lambda k: pl.BlockSpec((1, 1, d), lambda i: (i // tps, 0, k))
        tab_spec = pl.BlockSpec((tm, 128), lambda i: (i % tps, 0))
        hk = QK_NOPE_DIM + 128
        qk_shapes = [jax.ShapeDtypeStruct((N_HEADS, n_tok, hk), BF16),
                     jax.ShapeDtypeStruct((N_HEADS, V_HEAD_DIM, n_tok), BF16),
                     jax.ShapeDtypeStruct((N_HEADS, hk, n_tok), BF16)]
        qk_specs = [pl.BlockSpec((N_HEADS, tm, hk), lambda i: (0, i, 0)),
                    pl.BlockSpec((N_HEADS, V_HEAD_DIM, tm), lambda i: (0, 0, i)),
                    pl.BlockSpec((N_HEADS, hk, tm), lambda i: (0, 0, i))]
        cos_t, sin_t, w_uq_t, w_uk_all, w_uv_t_all = q_operands
        tab_t_spec = pl.BlockSpec((cos_t.shape[0], tm), lambda i: (0, i % tps))
        q_in_specs = [tab_t_spec, tab_t_spec, _resident(w_uq_t.shape, lambda i: (0, 0)),
                      _resident(w_uk_all.shape, lambda i: (0, 0)), _resident(w_uv_t_all.shape, lambda i: (0, 0))]
    vec = lambda n: pl.BlockSpec((1, n), lambda i: (0, 0))
    return pl.pallas_call(
        functools.partial(_mla_in_body, sample=sample),
        grid=(n_i,),
        in_specs=[
            x_spec, mod_spec(0), mod_spec(1), vec(d), vec(d),
            _resident(w_down_ext.shape, lambda i: (0, 0)),
            vec(lat), tab_spec, tab_spec,
            _resident(w_in.shape, lambda i: (0, 0)),
            vec(g_q.shape[0]),
        ] + q_in_specs,
        out_specs=[row_spec(lat), row_spec(rope), row_spec(d_attn)] + qk_specs,
        out_shape=[jax.ShapeDtypeStruct((n_tok, lat), F32), jax.ShapeDtypeStruct((n_tok, rope), F32),
                   jax.ShapeDtypeStruct((n_tok, d_attn), F32)] + qk_shapes,
        compiler_params=_params(("arbitrary",), 58),
        name="mla_in_sample" if sample else "mla_in_prompt",
    )(x, mod, mod, g_pre.reshape(1, d), kv_g_in.reshape(1, d), w_down_ext, kv_g_latent.reshape(1, lat),
      cos_tab, sin_tab, w_in, g_q.reshape(1, -1), *q_operands)


LANES = 128


def _softmax_step(s, v, m_scr, l_scr, acc_scr):
    n_chunks = s.shape[1] // LANES
    chunks = [s[:, c * LANES:(c + 1) * LANES] for c in range(n_chunks)]
    m_cur = chunks[0]
    for ch in chunks[1:]:
        m_cur = jnp.maximum(m_cur, ch)
    m_prev = m_scr[...]
    m_new = jnp.maximum(m_prev, jnp.max(m_cur, axis=1, keepdims=True))
    alpha = jnp.exp(m_prev - m_new)
    ps = [jnp.exp(ch - m_new) for ch in chunks]
    p_sum = ps[0]
    for p in ps[1:]:
        p_sum = p_sum + p
    l_scr[...] = alpha * l_scr[...] + p_sum
    m_scr[...] = m_new
    pv = _dot(jnp.concatenate(ps, axis=1).astype(BF16), v)
    for c in range(acc_scr.shape[1] // LANES):
        sl = slice(c * LANES, (c + 1) * LANES)
        acc_scr[:, sl] = alpha * acc_scr[:, sl] + pv[:, sl]


def _softmax_init(m_scr, l_scr, acc_scr):
    m_scr[...] = jnp.full(m_scr.shape, NEG_INF, F32)
    l_scr[...] = jnp.zeros(l_scr.shape, F32)
    acc_scr[...] = jnp.zeros(acc_scr.shape, F32)


def _softmax_inv_denominator(l_scr):
    return 1.0 / jnp.sum(l_scr[...], axis=1, keepdims=True)


def _attn_prompt_body(qt_ref, k_ref, vt_ref, z_ref, y_ref, m_scr, l_scr, acc_scr, *, tq):
    t_seq = k_ref.shape[1]
    row = lax.broadcasted_iota(jnp.int32, (tq, tq), 0)
    col = lax.broadcasted_iota(jnp.int32, (tq, tq), 1)
    on_or_below_diagonal = row <= col

    def scores(qi, kj):
        s_t = _dot(k_ref[0, kj * tq:(kj + 1) * tq, :], qt_ref[0, :, qi * tq:(qi + 1) * tq])
        return jnp.where(on_or_below_diagonal, s_t, NEG_INF) if kj == qi else s_t

    pairs = [(qi, kj) for qi in range(t_seq // tq) for kj in range(qi + 1)]
    s_next = scores(*pairs[0])
    for idx, (qi, kj) in enumerate(pairs):
        s_t = s_next
        if idx + 1 < len(pairs):
            s_next = scores(*pairs[idx + 1])
        qs = slice(qi * tq, (qi + 1) * tq)
        ks = slice(kj * tq, (kj + 1) * tq)
        m_cur = jnp.max(s_t, axis=0, keepdims=True)
        if kj == 0:
            m_new = m_cur
        else:
            m_prev = m_scr[...]
            m_new = jnp.maximum(m_prev, m_cur)
            alpha = jnp.exp(m_prev - m_new)
        p_t = jnp.exp(s_t - m_new)
        p_sum = jnp.sum(p_t, axis=0, keepdims=True)
        pv_t = _dot(vt_ref[0, :, ks], p_t.astype(BF16))
        if kj == 0:
            l_scr[...] = p_sum
            acc_scr[...] = pv_t
        else:
            l_scr[...] = alpha * l_scr[...] + p_sum
            acc_scr[...] = alpha * acc_scr[...] + pv_t
        m_scr[...] = m_new
        if kj == qi:
            o = (acc_scr[...] * (1.0 / l_scr[...])).T
            y_ref[qs, :] = (o * _silu(z_ref[qs, :])).astype(BF16)


def _attn_prompt(q_t, k_heads, v_t, z, *, nseq):
    nh, hk, n_tok = q_t.shape
    vd = v_t.shape[1]
    t = n_tok // nseq
    tq = 512
    return pl.pallas_call(
        functools.partial(_attn_prompt_body, tq=tq),
        grid=(nseq, nh),
        in_specs=[
            pl.BlockSpec((1, hk, t), lambda b, h: (h, 0, b)),
            pl.BlockSpec((1, t, hk), lambda b, h: (h, b, 0)),
            pl.BlockSpec((1, vd, t), lambda b, h: (h, 0, b)),
            pl.BlockSpec((t, vd), lambda b, h: (b, h)),
        ],
        out_specs=pl.BlockSpec((t, vd), lambda b, h: (b, h)),
        out_shape=jax.ShapeDtypeStruct((n_tok, nh * vd), BF16),
        scratch_shapes=[pltpu.VMEM((1, tq), F32), pltpu.VMEM((1, tq), F32), pltpu.VMEM((vd, tq), F32)],
        compiler_params=_params(("arbitrary", "arbitrary"), 48),
        name="attn_prompt",
    )(q_t, k_heads, v_t, z)


def _attn_sample_body(pt_ref, q_ref, qp_ref, ckvn_ref, kpen_ref, z_ref, wuv_ref, ck_hbm, kpt_hbm, y_ref,
                      m_scr, l_scr, acc_scr, k_buf, kpt_buf, sem, *, n_pg, n_chunks):
    b = pl.program_id(0)
    page = ck_hbm.shape[1]
    rows, lat = acc_scr.shape
    t_new = ckvn_ref.shape[0]
    n_slots = k_buf.shape[0]
    ahead = n_slots - 1
    assert n_chunks % n_slots == 0 and ahead <= n_chunks

    def chunk_copies(seq, c):
        slot = c % n_slots
        copies = []
        for k in range(n_pg):
            pg = pt_ref[seq, c * n_pg + k]
            copies.append(pltpu.make_async_copy(
                ck_hbm.at[pg], k_buf.at[slot, pl.ds(k * page, page), :], sem.at[0, slot]))
            copies.append(pltpu.make_async_copy(kpt_hbm.at[pg], kpt_buf.at[slot, k], sem.at[1, slot]))
        return copies

    def start_chunk(seq, c):
        for cp in chunk_copies(seq, c):
            cp.start()

    @pl.when(b == 0)
    def _():
        for c in range(ahead):
            start_chunk(b, c)

    _softmax_init(m_scr, l_scr, acc_scr)
    q = q_ref[0].astype(BF16)
    qp = qp_ref[0].astype(BF16)

    def wait_and_score(c):
        for cp in chunk_copies(b, c):
            cp.wait()
        slot = c % n_slots
        kk = k_buf[slot].astype(BF16)
        kpt = jnp.concatenate([kpt_buf[slot, k] for k in range(n_pg)], axis=1).astype(BF16)
        return _dot_nt(q, kk) + _dot(qp, kpt), kk

    scored_next = wait_and_score(0)
    for c in range(n_chunks):
        nxt = c + ahead
        if nxt < n_chunks:
            start_chunk(b, nxt)
        else:
            @pl.when(b + 1 < pl.num_programs(0))
            def _():
                start_chunk(b + 1, nxt - n_chunks)
        s, kk = scored_next
        if c + 1 < n_chunks:
            scored_next = wait_and_score(c + 1)
        _softmax_step(s, kk, m_scr, l_scr, acc_scr)

    pad = page - t_new
    kn = jnp.concatenate([ckvn_ref[...], jnp.zeros((pad, lat), F32)], axis=0).astype(BF16)
    kpn = jnp.concatenate([kpen_ref[...], jnp.zeros((pad, kpen_ref.shape[1]), F32)], axis=0).astype(BF16)
    s = _dot_nt(q, kn) + _dot_nt(qp, kpn)
    t_q = lax.broadcasted_iota(jnp.int32, (rows, page), 0) & (t_new - 1)
    t_k = lax.broadcasted_iota(jnp.int32, (rows, page), 1)
    _softmax_step(jnp.where(t_k <= t_q, s, NEG_INF), kn, m_scr, l_scr, acc_scr)

    ol = (acc_scr[...] * _softmax_inv_denominator(l_scr)).astype(BF16)
    o_all = _dot(ol, wuv_ref[...])
    vd = wuv_ref.shape[1] // N_HEADS
    for hd in range(N_HEADS):
        o = o_all[hd * t_new:(hd + 1) * t_new, hd * vd:(hd + 1) * vd]
        y_ref[:, hd * vd:(hd + 1) * vd] = o * _silu(z_ref[:, hd * vd:(hd + 1) * vd])


def _attn_sample(qlat, qpe, cache_ckv, cache_kpe_t, page_table, ckv_new, kpe_new, z, w_uv_all):
    nseq, rows, lat = qlat.shape
    rope = qpe.shape[2]
    page = cache_ckv.shape[1]
    n_pages = page_table.shape[1]
    t_new = ckv_new.shape[0] // nseq
    d_attn = z.shape[1]
    n_pg = 16
    n_chunks = n_pages // n_pg
    n_slots = 4

    grid_spec = pltpu.PrefetchScalarGridSpec(
        num_scalar_prefetch=1,
        grid=(nseq,),
        in_specs=[
            pl.BlockSpec((1, rows, lat), lambda b, pt: (b, 0, 0)),
            pl.BlockSpec((1, rows, rope), lambda b, pt: (b, 0, 0)),
            pl.BlockSpec((t_new, lat), lambda b, pt: (b, 0)),
            pl.BlockSpec((t_new, rope), lambda b, pt: (b, 0)),
            pl.BlockSpec((t_new, d_attn), lambda b, pt: (b, 0)),
            _resident(w_uv_all.shape, lambda b, pt: (0, 0)),
            pl.BlockSpec(memory_space=pl.ANY),
            pl.BlockSpec(memory_space=pl.ANY),
        ],
        out_specs=pl.BlockSpec((t_new, d_attn), lambda b, pt: (b, 0)),
        scratch_shapes=[pltpu.VMEM((rows, LANES), F32), pltpu.VMEM((rows, LANES), F32),
                        pltpu.VMEM((rows, lat), F32),
                        pltpu.VMEM((n_slots, n_pg * page, lat), F32),
                        pltpu.VMEM((n_slots, n_pg, rope, page), F32),
                        pltpu.SemaphoreType.DMA((2, n_slots))],
    )
    return pl.pallas_call(
        functools.partial(_attn_sample_body, n_pg=n_pg, n_chunks=n_chunks),
        grid_spec=grid_spec,
        out_shape=jax.ShapeDtypeStruct((nseq * t_new, d_attn), F32),
        compiler_params=_params(("arbitrary",), 48),
        name="attn_sample",
    )(page_table, qlat, qpe, ckv_new, kpe_new, z, w_uv_all, cache_ckv, cache_kpe_t)


def _rope_tables(pos):
    half = QK_ROPE_DIM // 2
    lane = jnp.arange(2 * QK_ROPE_DIM)
    inv_freq = ROPE_THETA ** (-(lane % half).astype(F32) / half)
    ang = pos.astype(F32)[:, None] * inv_freq[None, :]
    sign = jnp.where((lane // half) % 2 == 0, -1.0, 1.0).astype(F32)
    return jnp.cos(ang), jnp.sin(ang) * sign[None, :]


def _rope_tables_transposed(pos):
    half = QK_ROPE_DIM // 2
    inv_freq = ROPE_THETA ** (-jnp.arange(half, dtype=F32) / half)
    ang = inv_freq[:, None] * pos.astype(F32)[None, :]
    return jnp.cos(ang), jnp.sin(ang)


def _swap_halves(w, half):
    return jnp.concatenate([w[..., half:], w[..., :half]], axis=-1)


def kernel(x_prompt, x_sample, state_conv, cache_ckv, cache_kpe, page_table, c_prompt, c_sample, w_ada, b_ada, g_pre, g_post, conv_w_in, conv_w, conv_w_out, kv_g_in, kv_w_down, kv_g_latent, kv_w_up, mla_w_in, mla_g_q, mla_w_uq, mla_w_out):
    nb, seq, d = x_prompt.shape
    nd, t_dec, _ = x_sample.shape
    depth = w_ada.shape[0]
    n_a = conv_w_in.shape[0]
    assert depth == 2 and n_a == 1 and mla_w_in.shape[0] == 1
    lat = kv_g_latent.shape[0]
    rope = QK_ROPE_DIM
    half = rope // 2
    past_len = page_table.shape[1] * cache_ckv.shape[1]

    n_c = nb + nd
    n_c_pad = -(-n_c // 8) * 8
    c_all = jnp.concatenate([c_prompt, c_sample, jnp.zeros((n_c_pad - n_c, d), F32)], axis=0)
    mod = _ada(c_all, w_ada, b_ada)
    mod_p = [mod[l, :nb].reshape(nb, 1, 3 * d) for l in range(depth)]
    mod_s = [mod[l, nb:n_c].reshape(nd, 1, 3 * d) for l in range(depth)]

    w_in0 = conv_w_in[0].astype(BF16)
    w_out0 = conv_w_out[0].astype(BF16)
    y_p, conv_p = _conv_in(x_prompt, mod_p[0], g_pre[0], None, w_in0, conv_w[0], sample=False)
    xp = _out_proj(y_p, x_prompt, mod_p[0], g_post[0], w_out0, name="out_proj_conv_prompt")
    y_s, conv_s = _conv_in(x_sample, mod_s[0], g_pre[0], state_conv[0], w_in0, conv_w[0], sample=True)
    xs = _out_proj(y_s, x_sample, mod_s[0], g_post[0], w_out0, name="out_proj_conv_sample")

    w_kpe = kv_w_down[:, lat:]
    w_kpe_sw = _swap_halves(w_kpe, half)
    w_down_ext = jnp.concatenate([kv_w_down[:, :lat], w_kpe, w_kpe, w_kpe_sw, w_kpe_sw], axis=1).astype(BF16)
    qr = mla_g_q.shape[1]
    w_uq = mla_w_uq[0].reshape(qr, N_HEADS, QK_NOPE_DIM + rope)
    w_uq_pe = w_uq[:, :, QK_NOPE_DIM:]
    w_uq_ext = jnp.concatenate(
        [w_uq[:, :, :QK_NOPE_DIM].reshape(qr, -1), w_uq_pe.reshape(qr, -1),
         _swap_halves(w_uq_pe, half).reshape(qr, -1)], axis=1).astype(BF16)
    w_uk_t = jnp.transpose(kv_w_up[:, :, :QK_NOPE_DIM], (1, 2, 0)).astype(BF16)
    w_uk_all = kv_w_up[:, :, :QK_NOPE_DIM].reshape(lat, -1).astype(BF16)
    w_uv_all = kv_w_up[:, :, QK_NOPE_DIM:].reshape(lat, -1).astype(BF16)
    w_uv_t_all = jnp.transpose(kv_w_up[:, :, QK_NOPE_DIM:], (1, 2, 0)).reshape(-1, lat).astype(BF16)
    w_uq_t = mla_w_uq[0].T.astype(BF16)
    w_in1 = mla_w_in[0].astype(BF16)
    w_out1 = mla_w_out[0].astype(BF16)

    cos_p, sin_p = _rope_tables(jnp.arange(seq))
    cos_s, sin_s = _rope_tables(past_len + jnp.arange(t_dec))
    reps = 256 // t_dec
    cos_s, sin_s = jnp.tile(cos_s, (reps, 1)), jnp.tile(sin_s, (reps, 1))
    cos_t, sin_t = _rope_tables_transposed(jnp.arange(seq))

    ckv_p, kpe_p, z_p, k_heads, v_t, q_t = _mla_in(
        xp, mod_p[1], g_pre[1], kv_g_in, w_down_ext, kv_g_latent, cos_p, sin_p, w_in1, mla_g_q[0],
        (cos_t, sin_t, w_uq_t, w_uk_all, w_uv_t_all), sample=False)
    y_p = _attn_prompt(q_t, k_heads, v_t, z_p, nseq=nb)
    xp = _out_proj(y_p, xp, mod_p[1], g_post[1], w_out1, name="out_proj_mla_prompt")

    ckv_s, kpe_s, z_s, qlat_s, qpe_s = _mla_in(
        xs, mod_s[1], g_pre[1], kv_g_in, w_down_ext, kv_g_latent, cos_s, sin_s, w_in1, mla_g_q[0],
        (w_uq_ext, w_uk_t), sample=True)
    cache_kpe_t = jnp.swapaxes(cache_kpe, 1, 2)
    y_s = _attn_sample(qlat_s, qpe_s, cache_ckv, cache_kpe_t, page_table, ckv_s, kpe_s, z_s, w_uv_all)
    xs = _out_proj(y_s, xs, mod_s[1], g_post[1], w_out1, name="out_proj_mla_sample")

    return (xp, xs, conv_p[None], conv_s[None],
            ckv_p.reshape(nb, seq, lat), kpe_p.reshape(nb, seq, rope),
            ckv_s.reshape(nd, t_dec, lat), kpe_s.reshape(nd, t_dec, rope))
```

```python
import functools

import jax
import jax.numpy as jnp
from jax import lax
from jax.experimental import pallas as pl
from jax.experimental.pallas import tpu as pltpu

F32 = jnp.float32
BF16 = jnp.bfloat16
RMS_EPS = 1e-6
ROPE_THETA = 10000.0
MIB = 1024 * 1024
NEG_INF = float("-inf")

N_HEADS = 16
QK_NOPE_DIM = 128
QK_ROPE_DIM = 64
V_HEAD_DIM = 128
SOFTMAX_SCALE = (QK_NOPE_DIM + QK_ROPE_DIM) ** -0.5
LOG2_E = 1.4426950408889634
ONES_ROWS = 16


def _dot(a, b):
    return jnp.dot(a, b, preferred_element_type=F32)


def _dot_nt(a, b):
    return lax.dot_general(a, b, (((1,), (1,)), ((), ())), preferred_element_type=F32)


def _silu(x):
    return x * jax.nn.sigmoid(x)


def _unit_rms(x):
    return x * lax.rsqrt(jnp.mean(x * x, axis=-1, keepdims=True) + RMS_EPS)


def _params(semantics, vmem_mib):
    return pltpu.CompilerParams(dimension_semantics=semantics, vmem_limit_bytes=vmem_mib * MIB)


def _resident(block_shape, index_map):
    return pl.BlockSpec(block_shape, index_map, pipeline_mode=pl.Buffered(1))


def _ada_body(c_ref, w_ref, b_ref, o_ref):
    a = _silu(c_ref[...]).astype(BF16)
    o_ref[0] = _dot(a, w_ref[0].astype(BF16)) + b_ref[0]


def _ada(c_all, w_ada, b_ada):
    depth, d, d3 = w_ada.shape
    m = c_all.shape[0]
    tn = 1024
    return pl.pallas_call(
        _ada_body,
        grid=(depth, d3 // tn),
        in_specs=[
            pl.BlockSpec((m, d), lambda l, j: (0, 0)),
            pl.BlockSpec((1, d, tn), lambda l, j: (l, 0, j)),
            pl.BlockSpec((1, 1, tn), lambda l, j: (l, 0, j)),
        ],
        out_specs=pl.BlockSpec((1, m, tn), lambda l, j: (l, 0, j)),
        out_shape=jax.ShapeDtypeStruct((depth, m, d3), F32),
        compiler_params=_params(("arbitrary", "arbitrary"), 40),
        name="ada",
    )(c_all, w_ada, b_ada.reshape(depth, 1, d3))


CONV_COL_SPLIT = 2


def _conv_in_body(*refs, sample, tiles_per_seq):
    if sample:
        (x_ref, shift_ref, scale_ref, gpre_ref, st_ref, wb_ref, wc_ref, wv_ref, wz_ref, wconv_ref,
         y_ref, cs_ref, h_scr) = refs
    else:
        (x_ref, shift_ref, scale_ref, gpre_ref, wb_ref, wc_ref, wv_ref, wz_ref, wconv_ref,
         y_ref, cs_ref, h_scr, carry_scr) = refs
    i = pl.program_id(0)
    j = pl.program_id(1)
    g, r, d = x_ref.shape
    rows = g * r

    @pl.when(j == 0)
    def _():
        xn = _unit_rms(x_ref[...])
        h = xn * gpre_ref[...][None] * (1.0 + scale_ref[...]) + shift_ref[...]
        h_scr[...] = h.reshape(rows, d).astype(BF16)

    if not sample:
        @pl.when(i % tiles_per_seq == 0)
        def _():
            carry_scr[j] = jnp.zeros(carry_scr.shape[1:], F32)

    h = h_scr[...]
    tn = y_ref.shape[1]
    tc = tn // CONV_COL_SPLIT
    for cb in range(CONV_COL_SPLIT):
        cols = slice(cb * tc, (cb + 1) * tc)
        bg = _dot(h, wb_ref[:, cols])
        cg = _dot(h, wc_ref[:, cols])
        v = _dot(h, wv_ref[:, cols])
        z = _dot(h, wz_ref[:, cols])
        u = cg * v
        t = lax.broadcasted_iota(jnp.int32, (rows, tc), 0)
        if sample:
            t = t & (r - 1)
            st = st_ref[:, :, cols]
            s0 = jnp.broadcast_to(st[:, 0:1, :], (g, r, tc)).reshape(rows, tc)
            s1 = jnp.broadcast_to(st[:, 1:2, :], (g, r, tc)).reshape(rows, tc)
        else:
            carry = carry_scr[j, :, cols]
            s0 = carry[6:7, :]
            s1 = carry[7:8, :]
        p1 = jnp.where(t == 0, s1, pltpu.roll(u, 1, 0))
        p2 = jnp.where(t == 0, s0, jnp.where(t == 1, s1, pltpu.roll(u, 2, 0)))
        w = wconv_ref[:, cols]
        conv = w[0:1, :] * p2 + w[1:2, :] * p1 + w[2:3, :] * u
        y_ref[:, cols] = (bg * conv * _silu(z)).astype(BF16)
        if sample:
            cs_ref[:, :, cols] = u.reshape(g, r, tc)[:, r - 2:r, :]
        else:
            tail = u[rows - 8:rows, :]
            carry_scr[j, :, cols] = tail
            cs_ref[0, :, cols] = tail


def _conv_in(x, mod, g_pre, state, w_in, w_conv, *, sample):
    nseq, t, d = x.shape
    dc = w_conv.shape[1]
    tn = 512
    nj = dc // tn
    tm = 512
    if sample:
        g, r = tm // t, t
        n_i = nseq // g
        tps = 1
        x_spec = pl.BlockSpec((g, r, d), lambda i, j: (i, 0, 0))
        mod_spec = lambda k: pl.BlockSpec((g, 1, d), lambda i, j: (i, 0, k))
    else:
        g, r = 1, tm
        tps = t // tm
        n_i = nseq * tps
        x_spec = pl.BlockSpec((1, tm, d), lambda i, j: (i // tps, i % tps, 0))
        mod_spec = lambda k: pl.BlockSpec((1, 1, d), lambda i, j: (i // tps, 0, k))
    w_spec = lambda s: pl.BlockSpec((d, tn), lambda i, j: (0, s * nj + j))
    in_specs = [x_spec, mod_spec(0), mod_spec(1), pl.BlockSpec((1, d), lambda i, j: (0, 0))]
    args = [x, mod, mod, g_pre.reshape(1, d)]
    scratch = [pltpu.VMEM((g * r, d), BF16)]
    if sample:
        in_specs.append(pl.BlockSpec((g, 2, tn), lambda i, j: (i, 0, j)))
        args.append(state)
        cs_shape = jax.ShapeDtypeStruct((nseq, 2, dc), F32)
        cs_spec = pl.BlockSpec((g, 2, tn), lambda i, j: (i, 0, j))
    else:
        cs_shape = jax.ShapeDtypeStruct((n_i, 8, dc), F32)
        cs_spec = pl.BlockSpec((1, 8, tn), lambda i, j: (i, 0, j))
        scratch.append(pltpu.VMEM((nj, 8, tn), F32))
    in_specs += [w_spec(0), w_spec(1), w_spec(2), w_spec(3), pl.BlockSpec((3, tn), lambda i, j: (0, j))]
    args += [w_in, w_in, w_in, w_in, w_conv]
    y, cs = pl.pallas_call(
        functools.partial(_conv_in_body, sample=sample, tiles_per_seq=tps),
        grid=(n_i, nj),
        in_specs=in_specs,
        out_specs=[pl.BlockSpec((g * r, tn), lambda i, j: (i, j)), cs_spec],
        out_shape=[jax.ShapeDtypeStruct((nseq * t, dc), BF16), cs_shape],
        scratch_shapes=scratch,
        compiler_params=_params(("arbitrary", "arbitrary"), 56),
        name="conv_in_sample" if sample else "conv_in_prompt",
    )(*args)
    if not sample:
        cs = cs.reshape(nseq, tps, 8, dc)[:, tps - 1, 6:8, :]
    return y, cs


def _out_proj_body(y_ref, x_ref, gate_ref, gpost_ref, w_ref, o_ref):
    g, r, d = x_ref.shape
    o = _dot(y_ref[...].astype(BF16), w_ref[...])
    o = _unit_rms(o) * gpost_ref[...]
    o_ref[...] = x_ref[...] + gate_ref[...] * o.reshape(g, r, d)


def _out_proj(y, x, mod, g_post, w_out, *, name):
    nseq, t, d = x.shape
    tm = 512
    if t >= tm:
        g, r = 1, tm
        tps = t // tm
        x_spec = pl.BlockSpec((1, tm, d), lambda i: (i // tps, i % tps, 0))
        gate_spec = pl.BlockSpec((1, 1, d), lambda i: (i // tps, 0, 2))
    else:
        g, r = tm // t, t
        tps = 1
        x_spec = pl.BlockSpec((g, r, d), lambda i: (i, 0, 0))
        gate_spec = pl.BlockSpec((g, 1, d), lambda i: (i, 0, 2))
    n_i = nseq * t // tm
    return pl.pallas_call(
        _out_proj_body,
        grid=(n_i,),
        in_specs=[
            pl.BlockSpec((tm, y.shape[1]), lambda i: (i, 0)),
            x_spec,
            gate_spec,
            pl.BlockSpec((1, d), lambda i: (0, 0)),
            _resident(w_out.shape, lambda i: (0, 0)),
        ],
        out_specs=x_spec,
        out_shape=jax.ShapeDtypeStruct(x.shape, F32),
        compiler_params=_params(("arbitrary",), 48),
        name=name,
    )(y, x, mod, g_post.reshape(1, d), w_out)


def _mla_in_body(x_ref, shift_ref, scale_ref, gpre_ref, gkv_ref, wdown_ref, glat_ref, cos_ref, sin_ref,
                 win_ref, gq_ref, *refs, sample):
    if sample:
        wuq_ref, wuk_ref, ckv_ref, kpe_ref, z_ref, qlat_ref, qpe_ref = refs
    else:
        cost_ref, sint_ref, wuqt_ref, wuk_ref, wuvt_ref, ckv_ref, kpe_ref, z_ref, kh_ref, vt_ref, qt_ref = refs
    g, r, d = x_ref.shape
    rows = g * r
    lat = glat_ref.shape[1]
    rope = QK_ROPE_DIM
    xn = _unit_rms(x_ref[...])
    cos = cos_ref[...]
    sin = sin_ref[...]

    a = (xn * gkv_ref[...][None]).reshape(rows, d).astype(BF16)
    raw = _dot(a, wdown_ref[...])
    ckv = _unit_rms(raw[:, :lat]) * glat_ref[...]
    kpe2 = raw[:, lat:lat + 128] * cos + raw[:, lat + 128:lat + 256] * sin
    ckv_ref[...] = ckv
    kpe_ref[...] = kpe2[:, :rope]
    nope = QK_NOPE_DIM
    lane = lax.broadcasted_iota(jnp.int32, (rows, 128), 1)
    if not sample:
        ckv_b = ckv.astype(BF16)
        k_nope = _dot(ckv_b, wuk_ref[...])
        v_t = _dot_nt(wuvt_ref[...], ckv_b)
        kpe_pad = jnp.where(lane < rope, kpe2, 0.0).astype(BF16)
        vd = vt_ref.shape[1] - ONES_ROWS
        for hd in range(N_HEADS):
            kh_ref[hd, :, :nope] = k_nope[:, hd * nope:(hd + 1) * nope].astype(BF16)
            kh_ref[hd, :, nope:] = kpe_pad
            vt_ref[hd, :vd, :] = v_t[hd * vd:(hd + 1) * vd, :].astype(BF16)
            vt_ref[hd, vd:, :] = jnp.ones((ONES_ROWS, rows), BF16)

    h = xn * gpre_ref[...][None] * (1.0 + scale_ref[...]) + shift_ref[...]
    qz = _dot(h.reshape(rows, d).astype(BF16), win_ref[...])
    qr = gq_ref.shape[1]
    z_ref[...] = qz[:, qr:]
    qa = (_unit_rms(qz[:, :qr]) * gq_ref[...]).astype(BF16)
    if not sample:
        q_t = _dot_nt(wuqt_ref[...], qa) * (SOFTMAX_SCALE * LOG2_E)
        cos_t = cost_ref[...]
        sin_t = sint_ref[...]
        hq = nope + rope
        half = rope // 2
        for hd in range(N_HEADS):
            base = hd * hq
            x1 = q_t[base + nope:base + nope + half, :]
            x2 = q_t[base + nope + half:base + hq, :]
            qt_ref[hd, :nope, :] = q_t[base:base + nope, :].astype(BF16)
            qt_ref[hd, nope:nope + half, :] = (x1 * cos_t - x2 * sin_t).astype(BF16)
            qt_ref[hd, nope + half:hq, :] = (x2 * cos_t + x1 * sin_t).astype(BF16)
            qt_ref[hd, hq:, :] = jnp.zeros((qt_ref.shape[1] - hq, rows), BF16)
        return

    q = _dot(qa, wuq_ref[...]) * SOFTMAX_SCALE
    n_nope = N_HEADS * nope
    n_rope = N_HEADS * rope
    for hd in range(N_HEADS):
        qn = q[:, hd * nope:(hd + 1) * nope].astype(BF16)
        ql = _dot(qn, wuk_ref[hd])
        qlat_ref[:, hd * r:(hd + 1) * r, :] = ql.reshape(g, r, lat)
    for pr in range(N_HEADS // 2):
        lo = n_nope + pr * 128
        pe = q[:, lo:lo + 128] * cos + q[:, lo + n_rope:lo + n_rope + 128] * sin
        for k in range(2):
            hd = 2 * pr + k
            qpe_ref[:, hd * r:(hd + 1) * r, :] = pe[:, k * rope:(k + 1) * rope].reshape(g, r, rope)


def _mla_in(x, mod, g_pre, kv_g_in, w_down_ext, kv_g_latent, cos_tab, sin_tab, w_in, g_q, q_operands, *, sample):
    nseq, t, d = x.shape
    n_tok = nseq * t
    lat = kv_g_latent.shape[0]
    rope = QK_ROPE_DIM
    d_attn = w_in.shape[1] - g_q.shape[0]
    tm = 256
    n_i = n_tok // tm
    row_spec = lambda w: pl.BlockSpec((tm, w), lambda i: (i, 0))
    if sample:
        g, r = tm // t, t
        x_spec = pl.BlockSpec((g, r, d), lambda i: (i, 0, 0))
        mod_spec = lambda k: pl.BlockSpec((g, 1, d), lambda i: (i, 0, k))
        tab_spec = pl.BlockSpec((tm, 128), lambda i: (0, 0))
        qk_shapes = [jax.ShapeDtypeStruct((nseq, N_HEADS * t, lat), F32),
                     jax.ShapeDtypeStruct((nseq, N_HEADS * t, rope), F32)]
        qk_specs = [pl.BlockSpec((g, N_HEADS * t, lat), lambda i: (i, 0, 0)),
                    pl.BlockSpec((g, N_HEADS * t, rope), lambda i: (i, 0, 0))]
        w_uq_ext, w_uk_t = q_operands
        q_in_specs = [_resident(w_uq_ext.shape, lambda i: (0, 0)), _resident(w_uk_t.shape, lambda i: (0, 0, 0))]
    else:
        tps = t // tm
        x_spec = pl.BlockSpec((1, tm, d), lambda i: (i // tps, i % tps, 0))
        mod_spec = lambda k: pl.BlockSpec((1, 1, d), lambda i: (i // tps, 0, k))
        tab_spec = pl.BlockSpec((tm, 128), lambda i: (i % tps, 0))
        hk = QK_NOPE_DIM + 128
        qk_shapes = [jax.ShapeDtypeStruct((N_HEADS, n_tok, hk), BF16),
                     jax.ShapeDtypeStruct((N_HEADS, V_HEAD_DIM + ONES_ROWS, n_tok), BF16),
                     jax.ShapeDtypeStruct((N_HEADS, hk, n_tok), BF16)]
        qk_specs = [pl.BlockSpec((N_HEADS, tm, hk), lambda i: (0, i, 0)),
                    pl.BlockSpec((N_HEADS, V_HEAD_DIM + ONES_ROWS, tm), lambda i: (0, 0, i)),
                    pl.BlockSpec((N_HEADS, hk, tm), lambda i: (0, 0, i))]
        cos_t, sin_t, w_uq_t, w_uk_all, w_uv_t_all = q_operands
        tab_t_spec = pl.BlockSpec((cos_t.shape[0], tm), lambda i: (0, i % tps))
        q_in_specs = [tab_t_spec, tab_t_spec, _resident(w_uq_t.shape, lambda i: (0, 0)),
                      _resident(w_uk_all.shape, lambda i: (0, 0)), _resident(w_uv_t_all.shape, lambda i: (0, 0))]
    vec = lambda n: pl.BlockSpec((1, n), lambda i: (0, 0))
    return pl.pallas_call(
        functools.partial(_mla_in_body, sample=sample),
        grid=(n_i,),
        in_specs=[
            x_spec, mod_spec(0), mod_spec(1), vec(d), vec(d),
            _resident(w_down_ext.shape, lambda i: (0, 0)),
            vec(lat), tab_spec, tab_spec,
            _resident(w_in.shape, lambda i: (0, 0)),
            vec(g_q.shape[0]),
        ] + q_in_specs,
        out_specs=[row_spec(lat), row_spec(rope), row_spec(d_attn)] + qk_specs,
        out_shape=[jax.ShapeDtypeStruct((n_tok, lat), F32), jax.ShapeDtypeStruct((n_tok, rope), F32),
                   jax.ShapeDtypeStruct((n_tok, d_attn), F32)] + qk_shapes,
        compiler_params=_params(("arbitrary",), 58),
        name="mla_in_sample" if sample else "mla_in_prompt",
    )(x, mod, mod, g_pre.reshape(1, d), kv_g_in.reshape(1, d), w_down_ext, kv_g_latent.reshape(1, lat),
      cos_tab, sin_tab, w_in, g_q.reshape(1, -1), *q_operands)


LANES = 128


def _softmax_step(s, v, m_scr, l_scr, acc_scr):
    n_chunks = s.shape[1] // LANES
    chunks = [s[:, c * LANES:(c + 1) * LANES] for c in range(n_chunks)]
    m_cur = chunks[0]
    for ch in chunks[1:]:
        m_cur = jnp.maximum(m_cur, ch)
    m_prev = m_scr[...]
    m_new = jnp.maximum(m_prev, jnp.max(m_cur, axis=1, keepdims=True))
    alpha = jnp.exp(m_prev - m_new)
    ps = [jnp.exp(ch - m_new) for ch in chunks]
    p_sum = ps[0]
    for p in ps[1:]:
        p_sum = p_sum + p
    l_scr[...] = alpha * l_scr[...] + p_sum
    m_scr[...] = m_new
    pv = _dot(jnp.concatenate(ps, axis=1).astype(BF16), v)
    for c in range(acc_scr.shape[1] // LANES):
        sl = slice(c * LANES, (c + 1) * LANES)
        acc_scr[:, sl] = alpha * acc_scr[:, sl] + pv[:, sl]


def _softmax_init(m_scr, l_scr, acc_scr):
    m_scr[...] = jnp.full(m_scr.shape, NEG_INF, F32)
    l_scr[...] = jnp.zeros(l_scr.shape, F32)
    acc_scr[...] = jnp.zeros(acc_scr.shape, F32)


def _softmax_inv_denominator(l_scr):
    return 1.0 / jnp.sum(l_scr[...], axis=1, keepdims=True)


def _attn_prompt_body(qt_ref, k_ref, vt_ref, z_ref, y_ref, m_scr, acc_scr, *, tq):
    t_seq = k_ref.shape[1]
    row = lax.broadcasted_iota(jnp.int32, (tq, tq), 0)
    col = lax.broadcasted_iota(jnp.int32, (tq, tq), 1)
    on_or_below_diagonal = row <= col

    def scores(qi, kj):
        s_t = _dot(k_ref[0, kj * tq:(kj + 1) * tq, :], qt_ref[0, :, qi * tq:(qi + 1) * tq])
        return jnp.where(on_or_below_diagonal, s_t, NEG_INF) if kj == qi else s_t

    pairs = [(qi, kj) for qi in range(t_seq // tq) for kj in range(qi + 1)]
    s_next = scores(*pairs[0])
    for idx, (qi, kj) in enumerate(pairs):
        s_t = s_next
        if idx + 1 < len(pairs):
            s_next = scores(*pairs[idx + 1])
        qs = slice(qi * tq, (qi + 1) * tq)
        ks = slice(kj * tq, (kj + 1) * tq)
        m_cur = jnp.max(s_t, axis=0, keepdims=True)
        if kj == 0:
            m_new = m_cur
        else:
            m_prev = m_scr[...]
            m_new = jnp.maximum(m_prev, m_cur)
            alpha = jnp.exp2(m_prev - m_new)
        p_t = jnp.exp2(s_t - m_new).astype(BF16)
        pv_t = _dot(vt_ref[0, :, ks], p_t)
        if kj == 0:
            acc_scr[...] = pv_t
        else:
            acc_scr[...] = alpha * acc_scr[...] + pv_t
        m_scr[...] = m_new
        if kj == qi:
            vd = y_ref.shape[1]
            o = (acc_scr[:vd, :] * (1.0 / acc_scr[vd:vd + 1, :])).T
            y_ref[qs, :] = (o * _silu(z_ref[qs, :])).astype(BF16)


def _attn_prompt(q_t, k_heads, v_t, z, *, nseq):
    nh, hk, n_tok = q_t.shape
    v_rows = v_t.shape[1]
    vd = v_rows - ONES_ROWS
    t = n_tok // nseq
    tq = 512
    return pl.pallas_call(
        functools.partial(_attn_prompt_body, tq=tq),
        grid=(nseq, nh),
        in_specs=[
            pl.BlockSpec((1, hk, t), lambda b, h: (h, 0, b)),
            pl.BlockSpec((1, t, hk), lambda b, h: (h, b, 0)),
            pl.BlockSpec((1, v_rows, t), lambda b, h: (h, 0, b)),
            pl.BlockSpec((t, vd), lambda b, h: (b, h)),
        ],
        out_specs=pl.BlockSpec((t, vd), lambda b, h: (b, h)),
        out_shape=jax.ShapeDtypeStruct((n_tok, nh * vd), BF16),
        scratch_shapes=[pltpu.VMEM((1, tq), F32), pltpu.VMEM((v_rows, tq), F32)],
        compiler_params=_params(("arbitrary", "arbitrary"), 48),
        name="attn_prompt",
    )(q_t, k_heads, v_t, z)


def _attn_sample_body(pt_ref, q_ref, qp_ref, ckvn_ref, kpen_ref, z_ref, wuv_ref, ck_hbm, kpt_hbm, y_ref,
                      m_scr, l_scr, acc_scr, k_buf, kpt_buf, sem, *, n_pg, n_chunks):
    b = pl.program_id(0)
    page = ck_hbm.shape[1]
    rows, lat = acc_scr.shape
    t_new = ckvn_ref.shape[0]
    n_slots = k_buf.shape[0]
    ahead = n_slots - 1
    assert n_chunks % n_slots == 0 and ahead <= n_chunks

    def chunk_copies(seq, c):
        slot = c % n_slots
        copies = []
        for k in range(n_pg):
            pg = pt_ref[seq, c * n_pg + k]
            copies.append(pltpu.make_async_copy(
                ck_hbm.at[pg], k_buf.at[slot, pl.ds(k * page, page), :], sem.at[0, slot]))
            copies.append(pltpu.make_async_copy(kpt_hbm.at[pg], kpt_buf.at[slot, k], sem.at[1, slot]))
        return copies

    def start_chunk(seq, c):
        for cp in chunk_copies(seq, c):
            cp.start()

    @pl.when(b == 0)
    def _():
        for c in range(ahead):
            start_chunk(b, c)

    _softmax_init(m_scr, l_scr, acc_scr)
    q = q_ref[0].astype(BF16)
    qp = qp_ref[0].astype(BF16)

    def wait_and_score(c):
        for cp in chunk_copies(b, c):
            cp.wait()
        slot = c % n_slots
        kk = k_buf[slot].astype(BF16)
        kpt = jnp.concatenate([kpt_buf[slot, k] for k in range(n_pg)], axis=1).astype(BF16)
        return _dot_nt(q, kk) + _dot(qp, kpt), kk

    scored_next = wait_and_score(0)
    for c in range(n_chunks):
        nxt = c + ahead
        if nxt < n_chunks:
            start_chunk(b, nxt)
        else:
            @pl.when(b + 1 < pl.num_programs(0))
            def _():
                start_chunk(b + 1, nxt - n_chunks)
        s, kk = scored_next
        if c + 1 < n_chunks:
            scored_next = wait_and_score(c + 1)
        _softmax_step(s, kk, m_scr, l_scr, acc_scr)

    pad = page - t_new
    kn = jnp.concatenate([ckvn_ref[...], jnp.zeros((pad, lat), F32)], axis=0).astype(BF16)
    kpn = jnp.concatenate([kpen_ref[...], jnp.zeros((pad, kpen_ref.shape[1]), F32)], axis=0).astype(BF16)
    s = _dot_nt(q, kn) + _dot_nt(qp, kpn)
    t_q = lax.broadcasted_iota(jnp.int32, (rows, page), 0) & (t_new - 1)
    t_k = lax.broadcasted_iota(jnp.int32, (rows, page), 1)
    _softmax_step(jnp.where(t_k <= t_q, s, NEG_INF), kn, m_scr, l_scr, acc_scr)

    ol = (acc_scr[...] * _softmax_inv_denominator(l_scr)).astype(BF16)
    o_all = _dot(ol, wuv_ref[...])
    vd = wuv_ref.shape[1] // N_HEADS
    for hd in range(N_HEADS):
        o = o_all[hd * t_new:(hd + 1) * t_new, hd * vd:(hd + 1) * vd]
        y_ref[:, hd * vd:(hd + 1) * vd] = o * _silu(z_ref[:, hd * vd:(hd + 1) * vd])


def _attn_sample(qlat, qpe, cache_ckv, cache_kpe_t, page_table, ckv_new, kpe_new, z, w_uv_all):
    nseq, rows, lat = qlat.shape
    rope = qpe.shape[2]
    page = cache_ckv.shape[1]
    n_pages = page_table.shape[1]
    t_new = ckv_new.shape[0] // nseq
    d_attn = z.shape[1]
    n_pg = 16
    n_chunks = n_pages // n_pg
    n_slots = 4

    grid_spec = pltpu.PrefetchScalarGridSpec(
        num_scalar_prefetch=1,
        grid=(nseq,),
        in_specs=[
            pl.BlockSpec((1, rows, lat), lambda b, pt: (b, 0, 0)),
            pl.BlockSpec((1, rows, rope), lambda b, pt: (b, 0, 0)),
            pl.BlockSpec((t_new, lat), lambda b, pt: (b, 0)),
            pl.BlockSpec((t_new, rope), lambda b, pt: (b, 0)),
            pl.BlockSpec((t_new, d_attn), lambda b, pt: (b, 0)),
            _resident(w_uv_all.shape, lambda b, pt: (0, 0)),
            pl.BlockSpec(memory_space=pl.ANY),
            pl.BlockSpec(memory_space=pl.ANY),
        ],
        out_specs=pl.BlockSpec((t_new, d_attn), lambda b, pt: (b, 0)),
        scratch_shapes=[pltpu.VMEM((rows, LANES), F32), pltpu.VMEM((rows, LANES), F32),
                        pltpu.VMEM((rows, lat), F32),
                        pltpu.VMEM((n_slots, n_pg * page, lat), F32),
                        pltpu.VMEM((n_slots, n_pg, rope, page), F32),
                        pltpu.SemaphoreType.DMA((2, n_slots))],
    )
    return pl.pallas_call(
        functools.partial(_attn_sample_body, n_pg=n_pg, n_chunks=n_chunks),
        grid_spec=grid_spec,
        out_shape=jax.ShapeDtypeStruct((nseq * t_new, d_attn), F32),
        compiler_params=_params(("arbitrary",), 48),
        name="attn_sample",
    )(page_table, qlat, qpe, ckv_new, kpe_new, z, w_uv_all, cache_ckv, cache_kpe_t)


def _rope_tables(pos):
    half = QK_ROPE_DIM // 2
    lane = jnp.arange(2 * QK_ROPE_DIM)
    inv_freq = ROPE_THETA ** (-(lane % half).astype(F32) / half)
    ang = pos.astype(F32)[:, None] * inv_freq[None, :]
    sign = jnp.where((lane // half) % 2 == 0, -1.0, 1.0).astype(F32)
    return jnp.cos(ang), jnp.sin(ang) * sign[None, :]


def _rope_tables_transposed(pos):
    half = QK_ROPE_DIM // 2
    inv_freq = ROPE_THETA ** (-jnp.arange(half, dtype=F32) / half)
    ang = inv_freq[:, None] * pos.astype(F32)[None, :]
    return jnp.cos(ang), jnp.sin(ang)


def _swap_halves(w, half):
    return jnp.concatenate([w[..., half:], w[..., :half]], axis=-1)


def kernel(x_prompt, x_sample, state_conv, cache_ckv, cache_kpe, page_table, c_prompt, c_sample, w_ada, b_ada, g_pre, g_post, conv_w_in, conv_w, conv_w_out, kv_g_in, kv_w_down, kv_g_latent, kv_w_up, mla_w_in, mla_g_q, mla_w_uq, mla_w_out):
    nb, seq, d = x_prompt.shape
    nd, t_dec, _ = x_sample.shape
    depth = w_ada.shape[0]
    n_a = conv_w_in.shape[0]
    assert depth == 2 and n_a == 1 and mla_w_in.shape[0] == 1
    lat = kv_g_latent.shape[0]
    rope = QK_ROPE_DIM
    half = rope // 2
    past_len = page_table.shape[1] * cache_ckv.shape[1]

    n_c = nb + nd
    n_c_pad = -(-n_c // 8) * 8
    c_all = jnp.concatenate([c_prompt, c_sample, jnp.zeros((n_c_pad - n_c, d), F32)], axis=0)
    mod = _ada(c_all, w_ada, b_ada)
    mod_p = [mod[l, :nb].reshape(nb, 1, 3 * d) for l in range(depth)]
    mod_s = [mod[l, nb:n_c].reshape(nd, 1, 3 * d) for l in range(depth)]

    w_in0 = conv_w_in[0].astype(BF16)
    w_out0 = conv_w_out[0].astype(BF16)
    y_p, conv_p = _conv_in(x_prompt, mod_p[0], g_pre[0], None, w_in0, conv_w[0], sample=False)
    xp = _out_proj(y_p, x_prompt, mod_p[0], g_post[0], w_out0, name="out_proj_conv_prompt")
    y_s, conv_s = _conv_in(x_sample, mod_s[0], g_pre[0], state_conv[0], w_in0, conv_w[0], sample=True)
    xs = _out_proj(y_s, x_sample, mod_s[0], g_post[0], w_out0, name="out_proj_conv_sample")

    w_kpe = kv_w_down[:, lat:]
    w_kpe_sw = _swap_halves(w_kpe, half)
    w_down_ext = jnp.concatenate([kv_w_down[:, :lat], w_kpe, w_kpe, w_kpe_sw, w_kpe_sw], axis=1).astype(BF16)
    qr = mla_g_q.shape[1]
    w_uq = mla_w_uq[0].reshape(qr, N_HEADS, QK_NOPE_DIM + rope)
    w_uq_pe = w_uq[:, :, QK_NOPE_DIM:]
    w_uq_ext = jnp.concatenate(
        [w_uq[:, :, :QK_NOPE_DIM].reshape(qr, -1), w_uq_pe.reshape(qr, -1),
         _swap_halves(w_uq_pe, half).reshape(qr, -1)], axis=1).astype(BF16)
    w_uk_t = jnp.transpose(kv_w_up[:, :, :QK_NOPE_DIM], (1, 2, 0)).astype(BF16)
    w_uk_all = kv_w_up[:, :, :QK_NOPE_DIM].reshape(lat, -1).astype(BF16)
    w_uv_all = kv_w_up[:, :, QK_NOPE_DIM:].reshape(lat, -1).astype(BF16)
    w_uv_t_all = jnp.transpose(kv_w_up[:, :, QK_NOPE_DIM:], (1, 2, 0)).reshape(-1, lat).astype(BF16)
    w_uq_t = mla_w_uq[0].T.astype(BF16)
    w_in1 = mla_w_in[0].astype(BF16)
    w_out1 = mla_w_out[0].astype(BF16)

    cos_p, sin_p = _rope_tables(jnp.arange(seq))
    cos_s, sin_s = _rope_tables(past_len + jnp.arange(t_dec))
    reps = 256 // t_dec
    cos_s, sin_s = jnp.tile(cos_s, (reps, 1)), jnp.tile(sin_s, (reps, 1))
    cos_t, sin_t = _rope_tables_transposed(jnp.arange(seq))

    ckv_p, kpe_p, z_p, k_heads, v_t, q_t = _mla_in(
        xp, mod_p[1], g_pre[1], kv_g_in, w_down_ext, kv_g_latent, cos_p, sin_p, w_in1, mla_g_q[0],
        (cos_t, sin_t, w_uq_t, w_uk_all, w_uv_t_all), sample=False)
    y_p = _attn_prompt(q_t, k_heads, v_t, z_p, nseq=nb)
    xp = _out_proj(y_p, xp, mod_p[1], g_post[1], w_out1, name="out_proj_mla_prompt")

    ckv_s, kpe_s, z_s, qlat_s, qpe_s = _mla_in(
        xs, mod_s[1], g_pre[1], kv_g_in, w_down_ext, kv_g_latent, cos_s, sin_s, w_in1, mla_g_q[0],
        (w_uq_ext, w_uk_t), sample=True)
    cache_kpe_t = jnp.swapaxes(cache_kpe, 1, 2)
    y_s = _attn_sample(qlat_s, qpe_s, cache_ckv, cache_kpe_t, page_table, ckv_s, kpe_s, z_s, w_uv_all)
    xs = _out_proj(y_s, xs, mod_s[1], g_post[1], w_out1, name="out_proj_mla_sample")

    return (xp, xs, conv_p[None], conv_s[None],
            ckv_p.reshape(nb, seq, lat), kpe_p.reshape(nb, seq, rope),
            ckv_s.reshape(nd, t_dec, lat), kpe_s.reshape(nd, t_dec, rope))
```

```python
import functools

import jax
import jax.numpy as jnp
from jax import lax
from jax.experimental import pallas as pl
from jax.experimental.pallas import tpu as pltpu

F32 = jnp.float32
BF16 = jnp.bfloat16
RMS_EPS = 1e-6
ROPE_THETA = 10000.0
MIB = 1024 * 1024
NEG_INF = float("-inf")

N_HEADS = 16
QK_NOPE_DIM = 128
QK_ROPE_DIM = 64
V_HEAD_DIM = 128
SOFTMAX_SCALE = (QK_NOPE_DIM + QK_ROPE_DIM) ** -0.5
LOG2_E = 1.4426950408889634
ONES_ROWS = 16


def _dot(a, b):
    return jnp.dot(a, b, preferred_element_type=F32)


def _dot_nt(a, b):
    return lax.dot_general(a, b, (((1,), (1,)), ((), ())), preferred_element_type=F32)


def _silu(x):
    return x * jax.nn.sigmoid(x)


def _unit_rms(x):
    return x * lax.rsqrt(jnp.mean(x * x, axis=-1, keepdims=True) + RMS_EPS)


def _params(semantics, vmem_mib):
    return pltpu.CompilerParams(dimension_semantics=semantics, vmem_limit_bytes=vmem_mib * MIB)


def _resident(block_shape, index_map):
    return pl.BlockSpec(block_shape, index_map, pipeline_mode=pl.Buffered(1))


def _ada_body(c_ref, w_ref, b_ref, o_ref):
    a = _silu(c_ref[...]).astype(BF16)
    o_ref[0] = _dot(a, w_ref[0].astype(BF16)) + b_ref[0]


def _ada(c_all, w_ada, b_ada):
    depth, d, d3 = w_ada.shape
    m = c_all.shape[0]
    tn = 1024
    return pl.pallas_call(
        _ada_body,
        grid=(depth, d3 // tn),
        in_specs=[
            pl.BlockSpec((m, d), lambda l, j: (0, 0)),
            pl.BlockSpec((1, d, tn), lambda l, j: (l, 0, j)),
            pl.BlockSpec((1, 1, tn), lambda l, j: (l, 0, j)),
        ],
        out_specs=pl.BlockSpec((1, m, tn), lambda l, j: (l, 0, j)),
        out_shape=jax.ShapeDtypeStruct((depth, m, d3), F32),
        compiler_params=_params(("arbitrary", "arbitrary"), 40),
        name="ada",
    )(c_all, w_ada, b_ada.reshape(depth, 1, d3))


CONV_COL_SPLIT = 2


def _conv_in_body(*refs, sample, tiles_per_seq):
    if sample:
        (x_ref, shift_ref, scale_ref, gpre_ref, st_ref, wb_ref, wc_ref, wv_ref, wz_ref, wconv_ref,
         y_ref, cs_ref, h_scr) = refs
    else:
        (x_ref, shift_ref, scale_ref, gpre_ref, wb_ref, wc_ref, wv_ref, wz_ref, wconv_ref,
         y_ref, cs_ref, h_scr, carry_scr) = refs
    i = pl.program_id(0)
    j = pl.program_id(1)
    g, r, d = x_ref.shape
    rows = g * r

    @pl.when(j == 0)
    def _():
        xn = _unit_rms(x_ref[...])
        h = xn * gpre_ref[...][None] * (1.0 + scale_ref[...]) + shift_ref[...]
        h_scr[...] = h.reshape(rows, d).astype(BF16)

    if not sample:
        @pl.when(i % tiles_per_seq == 0)
        def _():
            carry_scr[j] = jnp.zeros(carry_scr.shape[1:], F32)

    h = h_scr[...]
    tn = y_ref.shape[1]
    tc = tn // CONV_COL_SPLIT
    for cb in range(CONV_COL_SPLIT):
        cols = slice(cb * tc, (cb + 1) * tc)
        bg = _dot(h, wb_ref[:, cols])
        cg = _dot(h, wc_ref[:, cols])
        v = _dot(h, wv_ref[:, cols])
        z = _dot(h, wz_ref[:, cols])
        u = cg * v
        t = lax.broadcasted_iota(jnp.int32, (rows, tc), 0)
        if sample:
            t = t & (r - 1)
            st = st_ref[:, :, cols]
            s0 = jnp.broadcast_to(st[:, 0:1, :], (g, r, tc)).reshape(rows, tc)
            s1 = jnp.broadcast_to(st[:, 1:2, :], (g, r, tc)).reshape(rows, tc)
        else:
            carry = carry_scr[j, :, cols]
            s0 = carry[6:7, :]
            s1 = carry[7:8, :]
        p1 = jnp.where(t == 0, s1, pltpu.roll(u, 1, 0))
        p2 = jnp.where(t == 0, s0, jnp.where(t == 1, s1, pltpu.roll(u, 2, 0)))
        w = wconv_ref[:, cols]
        conv = w[0:1, :] * p2 + w[1:2, :] * p1 + w[2:3, :] * u
        y_ref[:, cols] = (bg * conv * _silu(z)).astype(BF16)
        if sample:
            cs_ref[:, :, cols] = u.reshape(g, r, tc)[:, r - 2:r, :]
        else:
            tail = u[rows - 8:rows, :]
            carry_scr[j, :, cols] = tail
            cs_ref[0, :, cols] = tail


def _conv_in(x, mod, g_pre, state, w_in, w_conv, *, sample):
    nseq, t, d = x.shape
    dc = w_conv.shape[1]
    tn = 512
    nj = dc // tn
    tm = 512
    if sample:
        g, r = tm // t, t
        n_i = nseq // g
        tps = 1
        x_spec = pl.BlockSpec((g, r, d), lambda i, j: (i, 0, 0))
        mod_spec = lambda k: pl.BlockSpec((g, 1, d), lambda i, j: (i, 0, k))
    else:
        g, r = 1, tm
        tps = t // tm
        n_i = nseq * tps
        x_spec = pl.BlockSpec((1, tm, d), lambda i, j: (i // tps, i % tps, 0))
        mod_spec = lambda k: pl.BlockSpec((1, 1, d), lambda i, j: (i // tps, 0, k))
    w_spec = lambda s: pl.BlockSpec((d, tn), lambda i, j: (0, s * nj + j))
    in_specs = [x_spec, mod_spec(0), mod_spec(1), pl.BlockSpec((1, d), lambda i, j: (0, 0))]
    args = [x, mod, mod, g_pre.reshape(1, d)]
    scratch = [pltpu.VMEM((g * r, d), BF16)]
    if sample:
        in_specs.append(pl.BlockSpec((g, 2, tn), lambda i, j: (i, 0, j)))
        args.append(state)
        cs_shape = jax.ShapeDtypeStruct((nseq, 2, dc), F32)
        cs_spec = pl.BlockSpec((g, 2, tn), lambda i, j: (i, 0, j))
    else:
        cs_shape = jax.ShapeDtypeStruct((n_i, 8, dc), F32)
        cs_spec = pl.BlockSpec((1, 8, tn), lambda i, j: (i, 0, j))
        scratch.append(pltpu.VMEM((nj, 8, tn), F32))
    in_specs += [w_spec(0), w_spec(1), w_spec(2), w_spec(3), pl.BlockSpec((3, tn), lambda i, j: (0, j))]
    args += [w_in, w_in, w_in, w_in, w_conv]
    y, cs = pl.pallas_call(
        functools.partial(_conv_in_body, sample=sample, tiles_per_seq=tps),
        grid=(n_i, nj),
        in_specs=in_specs,
        out_specs=[pl.BlockSpec((g * r, tn), lambda i, j: (i, j)), cs_spec],
        out_shape=[jax.ShapeDtypeStruct((nseq * t, dc), BF16), cs_shape],
        scratch_shapes=scratch,
        compiler_params=_params(("arbitrary", "arbitrary"), 56),
        name="conv_in_sample" if sample else "conv_in_prompt",
    )(*args)
    if not sample:
        cs = cs.reshape(nseq, tps, 8, dc)[:, tps - 1, 6:8, :]
    return y, cs


def _out_proj_body(y_ref, x_ref, gate_ref, gpost_ref, w_ref, o_ref):
    g, r, d = x_ref.shape
    o = _dot(y_ref[...].astype(BF16), w_ref[...])
    o = _unit_rms(o) * gpost_ref[...]
    o_ref[...] = x_ref[...] + gate_ref[...] * o.reshape(g, r, d)


def _out_proj(y, x, mod, g_post, w_out, *, name):
    nseq, t, d = x.shape
    tm = 512
    if t >= tm:
        g, r = 1, tm
        tps = t // tm
        x_spec = pl.BlockSpec((1, tm, d), lambda i: (i // tps, i % tps, 0))
        gate_spec = pl.BlockSpec((1, 1, d), lambda i: (i // tps, 0, 2))
    else:
        g, r = tm // t, t
        tps = 1
        x_spec = pl.BlockSpec((g, r, d), lambda i: (i, 0, 0))
        gate_spec = pl.BlockSpec((g, 1, d), lambda i: (i, 0, 2))
    n_i = nseq * t // tm
    return pl.pallas_call(
        _out_proj_body,
        grid=(n_i,),
        in_specs=[
            pl.BlockSpec((tm, y.shape[1]), lambda i: (i, 0)),
            x_spec,
            gate_spec,
            pl.BlockSpec((1, d), lambda i: (0, 0)),
            _resident(w_out.shape, lambda i: (0, 0)),
        ],
        out_specs=x_spec,
        out_shape=jax.ShapeDtypeStruct(x.shape, F32),
        compiler_params=_params(("arbitrary",), 48),
        name=name,
    )(y, x, mod, g_post.reshape(1, d), w_out)


def _mla_in_body(x_ref, shift_ref, scale_ref, gpre_ref, gkv_ref, wdown_ref, glat_ref, cos_ref, sin_ref,
                 win_ref, gq_ref, *refs, sample):
    if sample:
        wuq_ref, wuk_ref, ckv_ref, kpe_ref, z_ref, qlat_ref, qpe_ref = refs
    else:
        cost_ref, sint_ref, wuqt_ref, wuk_ref, wuvt_ref, ckv_ref, kpe_ref, z_ref, kh_ref, vt_ref, qt_ref = refs
    g, r, d = x_ref.shape
    rows = g * r
    lat = glat_ref.shape[1]
    rope = QK_ROPE_DIM
    xn = _unit_rms(x_ref[...])
    cos = cos_ref[...]
    sin = sin_ref[...]

    a = (xn * gkv_ref[...][None]).reshape(rows, d).astype(BF16)
    raw = _dot(a, wdown_ref[...])
    ckv = _unit_rms(raw[:, :lat]) * glat_ref[...]
    kpe2 = raw[:, lat:lat + 128] * cos + raw[:, lat + 128:lat + 256] * sin
    ckv_ref[...] = ckv
    kpe_ref[...] = kpe2[:, :rope]
    nope = QK_NOPE_DIM
    lane = lax.broadcasted_iota(jnp.int32, (rows, 128), 1)
    if not sample:
        ckv_b = ckv.astype(BF16)
        k_nope = _dot(ckv_b, wuk_ref[...])
        v_t = _dot_nt(wuvt_ref[...], ckv_b)
        kpe_pad = jnp.where(lane < rope, kpe2, 0.0).astype(BF16)
        vd = vt_ref.shape[1] - ONES_ROWS
        for hd in range(N_HEADS):
            kh_ref[hd, :, :nope] = k_nope[:, hd * nope:(hd + 1) * nope].astype(BF16)
            kh_ref[hd, :, nope:] = kpe_pad
            vt_ref[hd, :vd, :] = v_t[hd * vd:(hd + 1) * vd, :].astype(BF16)
            vt_ref[hd, vd:, :] = jnp.ones((ONES_ROWS, rows), BF16)

    h = xn * gpre_ref[...][None] * (1.0 + scale_ref[...]) + shift_ref[...]
    qz = _dot(h.reshape(rows, d).astype(BF16), win_ref[...])
    qr = gq_ref.shape[1]
    z_ref[...] = qz[:, qr:]
    qa = (_unit_rms(qz[:, :qr]) * gq_ref[...]).astype(BF16)
    if not sample:
        q_t = _dot_nt(wuqt_ref[...], qa) * (SOFTMAX_SCALE * LOG2_E)
        cos_t = cost_ref[...]
        sin_t = sint_ref[...]
        hq = nope + rope
        half = rope // 2
        for hd in range(N_HEADS):
            base = hd * hq
            x1 = q_t[base + nope:base + nope + half, :]
            x2 = q_t[base + nope + half:base + hq, :]
            qt_ref[hd, :nope, :] = q_t[base:base + nope, :].astype(BF16)
            qt_ref[hd, nope:nope + half, :] = (x1 * cos_t - x2 * sin_t).astype(BF16)
            qt_ref[hd, nope + half:hq, :] = (x2 * cos_t + x1 * sin_t).astype(BF16)
            qt_ref[hd, hq:, :] = jnp.zeros((qt_ref.shape[1] - hq, rows), BF16)
        return

    q = _dot(qa, wuq_ref[...]) * SOFTMAX_SCALE
    n_nope = N_HEADS * nope
    n_rope = N_HEADS * rope
    for hd in range(N_HEADS):
        qn = q[:, hd * nope:(hd + 1) * nope].astype(BF16)
        ql = _dot(qn, wuk_ref[hd])
        qlat_ref[:, hd * r:(hd + 1) * r, :] = ql.reshape(g, r, lat)
    for pr in range(N_HEADS // 2):
        lo = n_nope + pr * 128
        pe = q[:, lo:lo + 128] * cos + q[:, lo + n_rope:lo + n_rope + 128] * sin
        for k in range(2):
            hd = 2 * pr + k
            qpe_ref[:, hd * r:(hd + 1) * r, :] = pe[:, k * rope:(k + 1) * rope].reshape(g, r, rope)


def _mla_in(x, mod, g_pre, kv_g_in, w_down_ext, kv_g_latent, cos_tab, sin_tab, w_in, g_q, q_operands, *, sample):
    nseq, t, d = x.shape
    n_tok = nseq * t
    lat = kv_g_latent.shape[0]
    rope = QK_ROPE_DIM
    d_attn = w_in.shape[1] - g_q.shape[0]
    tm = 256
    n_i = n_tok // tm
    row_spec = lambda w: pl.BlockSpec((tm, w), lambda i: (i, 0))
    if sample:
        g, r = tm // t, t
        x_spec = pl.BlockSpec((g, r, d), lambda i: (i, 0, 0))
        mod_spec = lambda k: pl.BlockSpec((g, 1, d), lambda i: (i, 0, k))
        tab_spec = pl.BlockSpec((tm, 128), lambda i: (0, 0))
        qk_shapes = [jax.ShapeDtypeStruct((nseq, N_HEADS * t, lat), F32),
                     jax.ShapeDtypeStruct((nseq, N_HEADS * t, rope), F32)]
        qk_specs = [pl.BlockSpec((g, N_HEADS * t, lat), lambda i: (i, 0, 0)),
                    pl.BlockSpec((g, N_HEADS * t, rope), lambda i: (i, 0, 0))]
        w_uq_ext, w_uk_t = q_operands
        q_in_specs = [_resident(w_uq_ext.shape, lambda i: (0, 0)), _resident(w_uk_t.shape, lambda i: (0, 0, 0))]
    else:
        tps = t // tm
        x_spec = pl.BlockSpec((1, tm, d), lambda i: (i // tps, i % tps, 0))
        mod_spec = lambda k: pl.BlockSpec((1, 1, d), lambda i: (i // tps, 0, k))
        tab_spec = pl.BlockSpec((tm, 128), lambda i: (i % tps, 0))
        hk = QK_NOPE_DIM + 128
        qk_shapes = [jax.ShapeDtypeStruct((N_HEADS, n_tok, hk), BF16),
                     jax.ShapeDtypeStruct((N_HEADS, V_HEAD_DIM + ONES_ROWS, n_tok), BF16),
                     jax.ShapeDtypeStruct((N_HEADS, hk, n_tok), BF16)]
        qk_specs = [pl.BlockSpec((N_HEADS, tm, hk), lambda i: (0, i, 0)),
                    pl.BlockSpec((N_HEADS, V_HEAD_DIM + ONES_ROWS, tm), lambda i: (0, 0, i)),
                    pl.BlockSpec((N_HEADS, hk, tm), lambda i: (0, 0, i))]
        cos_t, sin_t, w_uq_t, w_uk_all, w_uv_t_all = q_operands
        tab_t_spec = pl.BlockSpec((cos_t.shape[0], tm), lambda i: (0, i % tps))
        q_in_specs = [tab_t_spec, tab_t_spec, _resident(w_uq_t.shape, lambda i: (0, 0)),
                      _resident(w_uk_all.shape, lambda i: (0, 0)), _resident(w_uv_t_all.shape, lambda i: (0, 0))]
    vec = lambda n: pl.BlockSpec((1, n), lambda i: (0, 0))
    return pl.pallas_call(
        functools.partial(_mla_in_body, sample=sample),
        grid=(n_i,),
        in_specs=[
            x_spec, mod_spec(0), mod_spec(1), vec(d), vec(d),
            _resident(w_down_ext.shape, lambda i: (0, 0)),
            vec(lat), tab_spec, tab_spec,
            _resident(w_in.shape, lambda i: (0, 0)),
            vec(g_q.shape[0]),
        ] + q_in_specs,
        out_specs=[row_spec(lat), row_spec(rope), row_spec(d_attn)] + qk_specs,
        out_shape=[jax.ShapeDtypeStruct((n_tok, lat), F32), jax.ShapeDtypeStruct((n_tok, rope), F32),
                   jax.ShapeDtypeStruct((n_tok, d_attn), F32)] + qk_shapes,
        compiler_params=_params(("arbitrary",), 58),
        name="mla_in_sample" if sample else "mla_in_prompt",
    )(x, mod, mod, g_pre.reshape(1, d), kv_g_in.reshape(1, d), w_down_ext, kv_g_latent.reshape(1, lat),
      cos_tab, sin_tab, w_in, g_q.reshape(1, -1), *q_operands)


LANES = 128


def _softmax_step(s, v, m_scr, l_scr, acc_scr):
    n_chunks = s.shape[1] // LANES
    chunks = [s[:, c * LANES:(c + 1) * LANES] for c in range(n_chunks)]
    m_cur = chunks[0]
    for ch in chunks[1:]:
        m_cur = jnp.maximum(m_cur, ch)
    m_prev = m_scr[...]
    m_new = jnp.maximum(m_prev, jnp.max(m_cur, axis=1, keepdims=True))
    alpha = jnp.exp(m_prev - m_new)
    ps = [jnp.exp(ch - m_new) for ch in chunks]
    p_sum = ps[0]
    for p in ps[1:]:
        p_sum = p_sum + p
    l_scr[...] = alpha * l_scr[...] + p_sum
    m_scr[...] = m_new
    pv = _dot(jnp.concatenate(ps, axis=1).astype(BF16), v)
    for c in range(acc_scr.shape[1] // LANES):
        sl = slice(c * LANES, (c + 1) * LANES)
        acc_scr[:, sl] = alpha * acc_scr[:, sl] + pv[:, sl]


def _softmax_init(m_scr, l_scr, acc_scr):
    m_scr[...] = jnp.full(m_scr.shape, NEG_INF, F32)
    l_scr[...] = jnp.zeros(l_scr.shape, F32)
    acc_scr[...] = jnp.zeros(acc_scr.shape, F32)


def _softmax_inv_denominator(l_scr):
    return 1.0 / jnp.sum(l_scr[...], axis=1, keepdims=True)


def _prompt_pair_steps(qt_ref, k_ref, vt_ref, z_ref, y_ref, m_scr, acc_scr, tq):
    t_seq = k_ref.shape[1]
    row = lax.broadcasted_iota(jnp.int32, (tq, tq), 0)
    col = lax.broadcasted_iota(jnp.int32, (tq, tq), 1)
    on_or_below_diagonal = row <= col

    def scores(qi, kj):
        s_t = _dot(k_ref[0, kj * tq:(kj + 1) * tq, :], qt_ref[0, :, qi * tq:(qi + 1) * tq])
        return jnp.where(on_or_below_diagonal, s_t, NEG_INF) if kj == qi else s_t

    pairs = [(qi, kj) for qi in range(t_seq // tq) for kj in range(qi + 1)]
    pending = {}

    def make_step(idx, qi, kj):
        def step():
            s_t = pending.pop(idx) if idx in pending else scores(qi, kj)
            if idx + 1 < len(pairs):
                pending[idx + 1] = scores(*pairs[idx + 1])
            qs = slice(qi * tq, (qi + 1) * tq)
            ks = slice(kj * tq, (kj + 1) * tq)
            m_cur = jnp.max(s_t, axis=0, keepdims=True)
            if kj == 0:
                m_new = m_cur
            else:
                m_prev = m_scr[...]
                m_new = jnp.maximum(m_prev, m_cur)
                alpha = jnp.exp2(m_prev - m_new)
            p_t = jnp.exp2(s_t - m_new).astype(BF16)
            pv_t = _dot(vt_ref[0, :, ks], p_t)
            if kj == 0:
                acc_scr[...] = pv_t
            else:
                acc_scr[...] = alpha * acc_scr[...] + pv_t
            m_scr[...] = m_new
            if kj == qi:
                vd = y_ref.shape[1]
                o = (acc_scr[:vd, :] * (1.0 / acc_scr[vd:vd + 1, :])).T
                y_ref[qs, :] = (o * _silu(z_ref[qs, :])).astype(BF16)
        return step

    return [make_step(idx, qi, kj) for idx, (qi, kj) in enumerate(pairs)]


def _sample_sequence(b, k_in_step, n_seq, pt_ref, q_ref, qp_ref, ckvn_ref, kpen_ref, z_ref, wuv_ref, ck_hbm,
                     kpt_hbm, y_ref, m_scr, l_scr, acc_scr, k_buf, kpt_buf, sem, *, n_pg, n_chunks, t_new):
    page = ck_hbm.shape[1]
    rows, lat = acc_scr.shape
    new_rows = slice(k_in_step * t_new, (k_in_step + 1) * t_new)
    n_slots = k_buf.shape[0]
    ahead = n_slots - 1
    assert n_chunks % n_slots == 0 and ahead <= n_chunks

    def chunk_copies(seq, c):
        slot = c % n_slots
        copies = []
        for k in range(n_pg):
            pg = pt_ref[seq, c * n_pg + k]
            copies.append(pltpu.make_async_copy(
                ck_hbm.at[pg], k_buf.at[slot, pl.ds(k * page, page), :], sem.at[0, slot]))
            copies.append(pltpu.make_async_copy(kpt_hbm.at[pg], kpt_buf.at[slot, k], sem.at[1, slot]))
        return copies

    def start_chunk(seq, c):
        for cp in chunk_copies(seq, c):
            cp.start()

    @pl.when(b == 0)
    def _():
        for c in range(ahead):
            start_chunk(b, c)

    _softmax_init(m_scr, l_scr, acc_scr)
    q = q_ref[k_in_step].astype(BF16)
    qp = qp_ref[k_in_step].astype(BF16)

    def wait_and_score(c):
        for cp in chunk_copies(b, c):
            cp.wait()
        slot = c % n_slots
        kk = k_buf[slot].astype(BF16)
        kpt = jnp.concatenate([kpt_buf[slot, k] for k in range(n_pg)], axis=1).astype(BF16)
        return _dot_nt(q, kk) + _dot(qp, kpt), kk

    scored_next = wait_and_score(0)
    for c in range(n_chunks):
        nxt = c + ahead
        if nxt < n_chunks:
            start_chunk(b, nxt)
        else:
            @pl.when(b + 1 < n_seq)
            def _():
                start_chunk(b + 1, nxt - n_chunks)
        s, kk = scored_next
        if c + 1 < n_chunks:
            scored_next = wait_and_score(c + 1)
        _softmax_step(s, kk, m_scr, l_scr, acc_scr)

    pad = page - t_new
    kn = jnp.concatenate([ckvn_ref[new_rows, :], jnp.zeros((pad, lat), F32)], axis=0).astype(BF16)
    kpn = jnp.concatenate([kpen_ref[new_rows, :], jnp.zeros((pad, kpen_ref.shape[1]), F32)], axis=0).astype(BF16)
    s = _dot_nt(q, kn) + _dot_nt(qp, kpn)
    t_q = lax.broadcasted_iota(jnp.int32, (rows, page), 0) & (t_new - 1)
    t_k = lax.broadcasted_iota(jnp.int32, (rows, page), 1)
    _softmax_step(jnp.where(t_k <= t_q, s, NEG_INF), kn, m_scr, l_scr, acc_scr)

    ol = (acc_scr[...] * _softmax_inv_denominator(l_scr)).astype(BF16)
    o_all = _dot(ol, wuv_ref[...])
    vd = wuv_ref.shape[1] // N_HEADS
    for hd in range(N_HEADS):
        o = o_all[hd * t_new:(hd + 1) * t_new, hd * vd:(hd + 1) * vd]
        y_ref[new_rows, hd * vd:(hd + 1) * vd] = o * _silu(z_ref[new_rows, hd * vd:(hd + 1) * vd])


def _attn_body(pt_ref, qt_ref, k_ref, vt_ref, zp_ref, q_ref, qp_ref, ckvn_ref, kpen_ref, zs_ref, wuv_ref,
               ck_hbm, kpt_hbm, yp_ref, ys_ref, mp_scr, accp_scr, m_scr, l_scr, acc_scr, k_buf, kpt_buf, sem,
               *, tq, n_pg, n_chunks, seqs_per_step, t_new):
    step = pl.program_id(0) * pl.num_programs(1) + pl.program_id(1)
    n_seq = seqs_per_step * pl.num_programs(0) * pl.num_programs(1)
    prompt_steps = _prompt_pair_steps(qt_ref, k_ref, vt_ref, zp_ref, yp_ref, mp_scr, accp_scr, tq)
    per_seq = -(-len(prompt_steps) // seqs_per_step)
    for k in range(seqs_per_step):
        _sample_sequence(step * seqs_per_step + k, k, n_seq, pt_ref, q_ref, qp_ref, ckvn_ref, kpen_ref, zs_ref,
                         wuv_ref, ck_hbm, kpt_hbm, ys_ref, m_scr, l_scr, acc_scr, k_buf, kpt_buf, sem,
                         n_pg=n_pg, n_chunks=n_chunks, t_new=t_new)
        for prompt_step in prompt_steps[k * per_seq:(k + 1) * per_seq]:
            prompt_step()


def _attn(q_t, k_heads, v_t, z_p, qlat, qpe, cache_ckv, cache_kpe_t, page_table, ckv_new, kpe_new, z_s, w_uv_all,
          *, n_prompt_seq):
    nh, hk, n_tok = q_t.shape
    v_rows = v_t.shape[1]
    vd = v_rows - ONES_ROWS
    t = n_tok // n_prompt_seq
    tq = 512
    nseq, rows, lat = qlat.shape
    rope = qpe.shape[2]
    page = cache_ckv.shape[1]
    n_pages = page_table.shape[1]
    t_new = ckv_new.shape[0] // nseq
    d_attn = z_s.shape[1]
    n_pg = 16
    n_chunks = n_pages // n_pg
    n_slots = 4
    n_steps = n_prompt_seq * nh
    sps = nseq // n_steps
    assert sps * n_steps == nseq
    sample_rows = lambda w: pl.BlockSpec((sps * t_new, w), lambda b, h, pt: (b * nh + h, 0))

    grid_spec = pltpu.PrefetchScalarGridSpec(
        num_scalar_prefetch=1,
        grid=(n_prompt_seq, nh),
        in_specs=[
            pl.BlockSpec((1, hk, t), lambda b, h, pt: (h, 0, b)),
            pl.BlockSpec((1, t, hk), lambda b, h, pt: (h, b, 0)),
            pl.BlockSpec((1, v_rows, t), lambda b, h, pt: (h, 0, b)),
            pl.BlockSpec((t, vd), lambda b, h, pt: (b, h)),
            pl.BlockSpec((sps, rows, lat), lambda b, h, pt: (b * nh + h, 0, 0)),
            pl.BlockSpec((sps, rows, rope), lambda b, h, pt: (b * nh + h, 0, 0)),
            sample_rows(lat), sample_rows(rope), sample_rows(d_attn),
            _resident(w_uv_all.shape, lambda b, h, pt: (0, 0)),
            pl.BlockSpec(memory_space=pl.ANY),
            pl.BlockSpec(memory_space=pl.ANY),
        ],
        out_specs=[pl.BlockSpec((t, vd), lambda b, h, pt: (b, h)), sample_rows(d_attn)],
        scratch_shapes=[pltpu.VMEM((1, tq), F32), pltpu.VMEM((v_rows, tq), F32),
                        pltpu.VMEM((rows, LANES), F32), pltpu.VMEM((rows, LANES), F32),
                        pltpu.VMEM((rows, lat), F32),
                        pltpu.VMEM((n_slots, n_pg * page, lat), F32),
                        pltpu.VMEM((n_slots, n_pg, rope, page), F32),
                        pltpu.SemaphoreType.DMA((2, n_slots))],
    )
    return pl.pallas_call(
        functools.partial(_attn_body, tq=tq, n_pg=n_pg, n_chunks=n_chunks, seqs_per_step=sps, t_new=t_new),
        grid_spec=grid_spec,
        out_shape=[jax.ShapeDtypeStruct((n_tok, nh * vd), BF16),
                   jax.ShapeDtypeStruct((nseq * t_new, d_attn), F32)],
        compiler_params=_params(("arbitrary", "arbitrary"), 58),
        name="attn",
    )(page_table, q_t, k_heads, v_t, z_p, qlat, qpe, ckv_new, kpe_new, z_s, w_uv_all, cache_ckv, cache_kpe_t)


def _rope_tables(pos):
    half = QK_ROPE_DIM // 2
    lane = jnp.arange(2 * QK_ROPE_DIM)
    inv_freq = ROPE_THETA ** (-(lane % half).astype(F32) / half)
    ang = pos.astype(F32)[:, None] * inv_freq[None, :]
    sign = jnp.where((lane // half) % 2 == 0, -1.0, 1.0).astype(F32)
    return jnp.cos(ang), jnp.sin(ang) * sign[None, :]


def _rope_tables_transposed(pos):
    half = QK_ROPE_DIM // 2
    inv_freq = ROPE_THETA ** (-jnp.arange(half, dtype=F32) / half)
    ang = inv_freq[:, None] * pos.astype(F32)[None, :]
    return jnp.cos(ang), jnp.sin(ang)


def _swap_halves(w, half):
    return jnp.concatenate([w[..., half:], w[..., :half]], axis=-1)


def kernel(x_prompt, x_sample, state_conv, cache_ckv, cache_kpe, page_table, c_prompt, c_sample, w_ada, b_ada, g_pre, g_post, conv_w_in, conv_w, conv_w_out, kv_g_in, kv_w_down, kv_g_latent, kv_w_up, mla_w_in, mla_g_q, mla_w_uq, mla_w_out):
    nb, seq, d = x_prompt.shape
    nd, t_dec, _ = x_sample.shape
    depth = w_ada.shape[0]
    n_a = conv_w_in.shape[0]
    assert depth == 2 and n_a == 1 and mla_w_in.shape[0] == 1
    lat = kv_g_latent.shape[0]
    rope = QK_ROPE_DIM
    half = rope // 2
    past_len = page_table.shape[1] * cache_ckv.shape[1]

    n_c = nb + nd
    n_c_pad = -(-n_c // 8) * 8
    c_all = jnp.concatenate([c_prompt, c_sample, jnp.zeros((n_c_pad - n_c, d), F32)], axis=0)
    mod = _ada(c_all, w_ada, b_ada)
    mod_p = [mod[l, :nb].reshape(nb, 1, 3 * d) for l in range(depth)]
    mod_s = [mod[l, nb:n_c].reshape(nd, 1, 3 * d) for l in range(depth)]

    w_in0 = conv_w_in[0].astype(BF16)
    w_out0 = conv_w_out[0].astype(BF16)
    y_p, conv_p = _conv_in(x_prompt, mod_p[0], g_pre[0], None, w_in0, conv_w[0], sample=False)
    xp = _out_proj(y_p, x_prompt, mod_p[0], g_post[0], w_out0, name="out_proj_conv_prompt")
    y_s, conv_s = _conv_in(x_sample, mod_s[0], g_pre[0], state_conv[0], w_in0, conv_w[0], sample=True)
    xs = _out_proj(y_s, x_sample, mod_s[0], g_post[0], w_out0, name="out_proj_conv_sample")

    w_kpe = kv_w_down[:, lat:]
    w_kpe_sw = _swap_halves(w_kpe, half)
    w_down_ext = jnp.concatenate([kv_w_down[:, :lat], w_kpe, w_kpe, w_kpe_sw, w_kpe_sw], axis=1).astype(BF16)
    qr = mla_g_q.shape[1]
    w_uq = mla_w_uq[0].reshape(qr, N_HEADS, QK_NOPE_DIM + rope)
    w_uq_pe = w_uq[:, :, QK_NOPE_DIM:]
    w_uq_ext = jnp.concatenate(
        [w_uq[:, :, :QK_NOPE_DIM].reshape(qr, -1), w_uq_pe.reshape(qr, -1),
         _swap_halves(w_uq_pe, half).reshape(qr, -1)], axis=1).astype(BF16)
    w_uk_t = jnp.transpose(kv_w_up[:, :, :QK_NOPE_DIM], (1, 2, 0)).astype(BF16)
    w_uk_all = kv_w_up[:, :, :QK_NOPE_DIM].reshape(lat, -1).astype(BF16)
    w_uv_all = kv_w_up[:, :, QK_NOPE_DIM:].reshape(lat, -1).astype(BF16)
    w_uv_t_all = jnp.transpose(kv_w_up[:, :, QK_NOPE_DIM:], (1, 2, 0)).reshape(-1, lat).astype(BF16)
    w_uq_t = mla_w_uq[0].T.astype(BF16)
    w_in1 = mla_w_in[0].astype(BF16)
    w_out1 = mla_w_out[0].astype(BF16)

    cos_p, sin_p = _rope_tables(jnp.arange(seq))
    cos_s, sin_s = _rope_tables(past_len + jnp.arange(t_dec))
    reps = 256 // t_dec
    cos_s, sin_s = jnp.tile(cos_s, (reps, 1)), jnp.tile(sin_s, (reps, 1))
    cos_t, sin_t = _rope_tables_transposed(jnp.arange(seq))

    ckv_p, kpe_p, z_p, k_heads, v_t, q_t = _mla_in(
        xp, mod_p[1], g_pre[1], kv_g_in, w_down_ext, kv_g_latent, cos_p, sin_p, w_in1, mla_g_q[0],
        (cos_t, sin_t, w_uq_t, w_uk_all, w_uv_t_all), sample=False)
    ckv_s, kpe_s, z_s, qlat_s, qpe_s = _mla_in(
        xs, mod_s[1], g_pre[1], kv_g_in, w_down_ext, kv_g_latent, cos_s, sin_s, w_in1, mla_g_q[0],
        (w_uq_ext, w_uk_t), sample=True)
    cache_kpe_t = jnp.swapaxes(cache_kpe, 1, 2)
    y_p, y_s = _attn(q_t, k_heads, v_t, z_p, qlat_s, qpe_s, cache_ckv, cache_kpe_t, page_table, ckv_s, kpe_s, z_s,
                     w_uv_all, n_prompt_seq=nb)
    xp = _out_proj(y_p, xp, mod_p[1], g_post[1], w_out1, name="out_proj_mla_prompt")
    xs = _out_proj(y_s, xs, mod_s[1], g_post[1], w_out1, name="out_proj_mla_sample")

    return (xp, xs, conv_p[None], conv_s[None],
            ckv_p.reshape(nb, seq, lat), kpe_p.reshape(nb, seq, rope),
            ckv_s.reshape(nd, t_dec, lat), kpe_s.reshape(nd, t_dec, rope))
```

```python
import functools

import jax
import jax.numpy as jnp
from jax import lax
from jax.experimental import pallas as pl
from jax.experimental.pallas import tpu as pltpu

F32 = jnp.float32
BF16 = jnp.bfloat16
RMS_EPS = 1e-6
ROPE_THETA = 10000.0
MIB = 1024 * 1024
NEG_INF = float("-inf")

N_HEADS = 16
QK_NOPE_DIM = 128
QK_ROPE_DIM = 64
V_HEAD_DIM = 128
SOFTMAX_SCALE = (QK_NOPE_DIM + QK_ROPE_DIM) ** -0.5
LOG2_E = 1.4426950408889634
ONES_ROWS = 16


def _dot(a, b):
    return jnp.dot(a, b, preferred_element_type=F32)


def _dot_nt(a, b):
    return lax.dot_general(a, b, (((1,), (1,)), ((), ())), preferred_element_type=F32)


def _silu(x):
    return x * jax.nn.sigmoid(x)


def _unit_rms(x):
    return x * lax.rsqrt(jnp.mean(x * x, axis=-1, keepdims=True) + RMS_EPS)


def _params(semantics, vmem_mib):
    return pltpu.CompilerParams(dimension_semantics=semantics, vmem_limit_bytes=vmem_mib * MIB)


def _resident(block_shape, index_map):
    return pl.BlockSpec(block_shape, index_map, pipeline_mode=pl.Buffered(1))


def _ada_body(c_ref, w_ref, b_ref, o_ref):
    a = _silu(c_ref[...]).astype(BF16)
    o_ref[0] = _dot(a, w_ref[0].astype(BF16)) + b_ref[0]


def _ada(c_all, w_ada, b_ada):
    depth, d, d3 = w_ada.shape
    m = c_all.shape[0]
    tn = 1024
    return pl.pallas_call(
        _ada_body,
        grid=(depth, d3 // tn),
        in_specs=[
            pl.BlockSpec((m, d), lambda l, j: (0, 0)),
            pl.BlockSpec((1, d, tn), lambda l, j: (l, 0, j)),
            pl.BlockSpec((1, 1, tn), lambda l, j: (l, 0, j)),
        ],
        out_specs=pl.BlockSpec((1, m, tn), lambda l, j: (l, 0, j)),
        out_shape=jax.ShapeDtypeStruct((depth, m, d3), F32),
        compiler_params=_params(("arbitrary", "arbitrary"), 40),
        name="ada",
    )(c_all, w_ada, b_ada.reshape(depth, 1, d3))


CONV_COL_SPLIT = 2


def _conv_in_body(*refs, sample, tiles_per_seq):
    if sample:
        (x_ref, shift_ref, scale_ref, gpre_ref, st_ref, wb_ref, wc_ref, wv_ref, wz_ref, wconv_ref,
         y_ref, cs_ref, h_scr) = refs
    else:
        (x_ref, shift_ref, scale_ref, gpre_ref, wb_ref, wc_ref, wv_ref, wz_ref, wconv_ref,
         y_ref, cs_ref, h_scr, carry_scr) = refs
    i = pl.program_id(0)
    j = pl.program_id(1)
    g, r, d = x_ref.shape
    rows = g * r

    @pl.when(j == 0)
    def _():
        xn = _unit_rms(x_ref[...])
        h = xn * gpre_ref[...][None] * (1.0 + scale_ref[...]) + shift_ref[...]
        h_scr[...] = h.reshape(rows, d).astype(BF16)

    if not sample:
        @pl.when(i % tiles_per_seq == 0)
        def _():
            carry_scr[j] = jnp.zeros(carry_scr.shape[1:], F32)

    h = h_scr[...]
    tn = y_ref.shape[1]
    tc = tn // CONV_COL_SPLIT
    for cb in range(CONV_COL_SPLIT):
        cols = slice(cb * tc, (cb + 1) * tc)
        bg = _dot(h, wb_ref[:, cols])
        cg = _dot(h, wc_ref[:, cols])
        v = _dot(h, wv_ref[:, cols])
        z = _dot(h, wz_ref[:, cols])
        u = cg * v
        t = lax.broadcasted_iota(jnp.int32, (rows, tc), 0)
        if sample:
            t = t & (r - 1)
            st = st_ref[:, :, cols]
            s0 = jnp.broadcast_to(st[:, 0:1, :], (g, r, tc)).reshape(rows, tc)
            s1 = jnp.broadcast_to(st[:, 1:2, :], (g, r, tc)).reshape(rows, tc)
        else:
            carry = carry_scr[j, :, cols]
            s0 = carry[6:7, :]
            s1 = carry[7:8, :]
        p1 = jnp.where(t == 0, s1, pltpu.roll(u, 1, 0))
        p2 = jnp.where(t == 0, s0, jnp.where(t == 1, s1, pltpu.roll(u, 2, 0)))
        w = wconv_ref[:, cols]
        conv = w[0:1, :] * p2 + w[1:2, :] * p1 + w[2:3, :] * u
        y_ref[:, cols] = (bg * conv * _silu(z)).astype(BF16)
        if sample:
            cs_ref[:, :, cols] = u.reshape(g, r, tc)[:, r - 2:r, :]
        else:
            tail = u[rows - 8:rows, :]
            carry_scr[j, :, cols] = tail
            cs_ref[0, :, cols] = tail


def _conv_in(x, mod, g_pre, state, w_in, w_conv, *, sample):
    nseq, t, d = x.shape
    dc = w_conv.shape[1]
    tn = 512
    nj = dc // tn
    tm = 512
    if sample:
        g, r = tm // t, t
        n_i = nseq // g
        tps = 1
        x_spec = pl.BlockSpec((g, r, d), lambda i, j: (i, 0, 0))
        mod_spec = lambda k: pl.BlockSpec((g, 1, d), lambda i, j: (i, 0, k))
    else:
        g, r = 1, tm
        tps = t // tm
        n_i = nseq * tps
        x_spec = pl.BlockSpec((1, tm, d), lambda i, j: (i // tps, i % tps, 0))
        mod_spec = lambda k: pl.BlockSpec((1, 1, d), lambda i, j: (i // tps, 0, k))
    w_spec = lambda s: pl.BlockSpec((d, tn), lambda i, j: (0, s * nj + j))
    in_specs = [x_spec, mod_spec(0), mod_spec(1), pl.BlockSpec((1, d), lambda i, j: (0, 0))]
    args = [x, mod, mod, g_pre.reshape(1, d)]
    scratch = [pltpu.VMEM((g * r, d), BF16)]
    if sample:
        in_specs.append(pl.BlockSpec((g, 2, tn), lambda i, j: (i, 0, j)))
        args.append(state)
        cs_shape = jax.ShapeDtypeStruct((nseq, 2, dc), F32)
        cs_spec = pl.BlockSpec((g, 2, tn), lambda i, j: (i, 0, j))
    else:
        cs_shape = jax.ShapeDtypeStruct((n_i, 8, dc), F32)
        cs_spec = pl.BlockSpec((1, 8, tn), lambda i, j: (i, 0, j))
        scratch.append(pltpu.VMEM((nj, 8, tn), F32))
    in_specs += [w_spec(0), w_spec(1), w_spec(2), w_spec(3), pl.BlockSpec((3, tn), lambda i, j: (0, j))]
    args += [w_in, w_in, w_in, w_in, w_conv]
    y, cs = pl.pallas_call(
        functools.partial(_conv_in_body, sample=sample, tiles_per_seq=tps),
        grid=(n_i, nj),
        in_specs=in_specs,
        out_specs=[pl.BlockSpec((g * r, tn), lambda i, j: (i, j)), cs_spec],
        out_shape=[jax.ShapeDtypeStruct((nseq * t, dc), BF16), cs_shape],
        scratch_shapes=scratch,
        compiler_params=_params(("arbitrary", "arbitrary"), 56),
        name="conv_in_sample" if sample else "conv_in_prompt",
    )(*args)
    if not sample:
        cs = cs.reshape(nseq, tps, 8, dc)[:, tps - 1, 6:8, :]
    return y, cs


def _out_proj_body(y_ref, x_ref, gate_ref, gpost_ref, w_ref, o_ref):
    g, r, d = x_ref.shape
    o = _dot(y_ref[...].astype(BF16), w_ref[...])
    o = _unit_rms(o) * gpost_ref[...]
    o_ref[...] = x_ref[...] + gate_ref[...] * o.reshape(g, r, d)


def _out_proj(y, x, mod, g_post, w_out, *, name):
    nseq, t, d = x.shape
    tm = 512
    if t >= tm:
        g, r = 1, tm
        tps = t // tm
        x_spec = pl.BlockSpec((1, tm, d), lambda i: (i // tps, i % tps, 0))
        gate_spec = pl.BlockSpec((1, 1, d), lambda i: (i // tps, 0, 2))
    else:
        g, r = tm // t, t
        tps = 1
        x_spec = pl.BlockSpec((g, r, d), lambda i: (i, 0, 0))
        gate_spec = pl.BlockSpec((g, 1, d), lambda i: (i, 0, 2))
    n_i = nseq * t // tm
    return pl.pallas_call(
        _out_proj_body,
        grid=(n_i,),
        in_specs=[
            pl.BlockSpec((tm, y.shape[1]), lambda i: (i, 0)),
            x_spec,
            gate_spec,
            pl.BlockSpec((1, d), lambda i: (0, 0)),
            _resident(w_out.shape, lambda i: (0, 0)),
        ],
        out_specs=x_spec,
        out_shape=jax.ShapeDtypeStruct(x.shape, F32),
        compiler_params=_params(("arbitrary",), 48),
        name=name,
    )(y, x, mod, g_post.reshape(1, d), w_out)


def _mla_in_body(x_ref, shift_ref, scale_ref, gpre_ref, gkv_ref, wdown_ref, glat_ref, cos_ref, sin_ref,
                 win_ref, gq_ref, *refs, sample):
    if sample:
        wuq_ref, wuk_ref, ckv_ref, kpe_ref, z_ref, qlat_ref, qpe_ref = refs
    else:
        cost_ref, sint_ref, wuqt_ref, wuk_ref, wuvt_ref, ckv_ref, kpe_ref, z_ref, kh_ref, vt_ref, qt_ref = refs
    g, r, d = x_ref.shape
    rows = g * r
    lat = glat_ref.shape[1]
    rope = QK_ROPE_DIM
    xn = _unit_rms(x_ref[...])
    cos = cos_ref[...]
    sin = sin_ref[...]

    a = (xn * gkv_ref[...][None]).reshape(rows, d).astype(BF16)
    raw = _dot(a, wdown_ref[...])
    ckv = _unit_rms(raw[:, :lat]) * glat_ref[...]
    kpe2 = raw[:, lat:lat + 128] * cos + raw[:, lat + 128:lat + 256] * sin
    ckv_ref[...] = ckv
    kpe_ref[...] = kpe2[:, :rope]
    nope = QK_NOPE_DIM
    lane = lax.broadcasted_iota(jnp.int32, (rows, 128), 1)
    if not sample:
        ckv_b = ckv.astype(BF16)
        k_nope = _dot(ckv_b, wuk_ref[...])
        v_t = _dot_nt(wuvt_ref[...], ckv_b)
        kpe_pad = jnp.where(lane < rope, kpe2, 0.0).astype(BF16)
        vd = vt_ref.shape[1] - ONES_ROWS
        for hd in range(N_HEADS):
            kh_ref[hd, :, :nope] = k_nope[:, hd * nope:(hd + 1) * nope].astype(BF16)
            kh_ref[hd, :, nope:] = kpe_pad
            vt_ref[hd, :vd, :] = v_t[hd * vd:(hd + 1) * vd, :].astype(BF16)
            vt_ref[hd, vd:, :] = jnp.ones((ONES_ROWS, rows), BF16)

    h = xn * gpre_ref[...][None] * (1.0 + scale_ref[...]) + shift_ref[...]
    qz = _dot(h.reshape(rows, d).astype(BF16), win_ref[...])
    qr = gq_ref.shape[1]
    z_ref[...] = qz[:, qr:]
    qa = (_unit_rms(qz[:, :qr]) * gq_ref[...]).astype(BF16)
    if not sample:
        q_t = _dot_nt(wuqt_ref[...], qa) * (SOFTMAX_SCALE * LOG2_E)
        cos_t = cost_ref[...]
        sin_t = sint_ref[...]
        hq = nope + rope
        half = rope // 2
        for hd in range(N_HEADS):
            base = hd * hq
            x1 = q_t[base + nope:base + nope + half, :]
            x2 = q_t[base + nope + half:base + hq, :]
            qt_ref[hd, :nope, :] = q_t[base:base + nope, :].astype(BF16)
            qt_ref[hd, nope:nope + half, :] = (x1 * cos_t - x2 * sin_t).astype(BF16)
            qt_ref[hd, nope + half:hq, :] = (x2 * cos_t + x1 * sin_t).astype(BF16)
            qt_ref[hd, hq:, :] = jnp.zeros((qt_ref.shape[1] - hq, rows), BF16)
        return

    q = _dot(qa, wuq_ref[...]) * SOFTMAX_SCALE
    n_nope = N_HEADS * nope
    n_rope = N_HEADS * rope
    for hd in range(N_HEADS):
        qn = q[:, hd * nope:(hd + 1) * nope].astype(BF16)
        ql = _dot(qn, wuk_ref[hd])
        qlat_ref[:, hd * r:(hd + 1) * r, :] = ql.reshape(g, r, lat)
    for pr in range(N_HEADS // 2):
        lo = n_nope + pr * 128
        pe = q[:, lo:lo + 128] * cos + q[:, lo + n_rope:lo + n_rope + 128] * sin
        for k in range(2):
            hd = 2 * pr + k
            qpe_ref[:, hd * r:(hd + 1) * r, :] = pe[:, k * rope:(k + 1) * rope].reshape(g, r, rope)


def _mla_in(x, mod, g_pre, kv_g_in, w_down_ext, kv_g_latent, cos_tab, sin_tab, w_in, g_q, q_operands, *, sample):
    nseq, t, d = x.shape
    n_tok = nseq * t
    lat = kv_g_latent.shape[0]
    rope = QK_ROPE_DIM
    d_attn = w_in.shape[1] - g_q.shape[0]
    tm = 256
    n_i = n_tok // tm
    row_spec = lambda w: pl.BlockSpec((tm, w), lambda i: (i, 0))
    if sample:
        g, r = tm // t, t
        x_spec = pl.BlockSpec((g, r, d), lambda i: (i, 0, 0))
        mod_spec = lambda k: pl.BlockSpec((g, 1, d), lambda i: (i, 0, k))
        tab_spec = pl.BlockSpec((tm, 128), lambda i: (0, 0))
        qk_shapes = [jax.ShapeDtypeStruct((nseq, N_HEADS * t, lat), F32),
                     jax.ShapeDtypeStruct((nseq, N_HEADS * t, rope), F32)]
        qk_specs = [pl.BlockSpec((g, N_HEADS * t, lat), lambda i: (i, 0, 0)),
                    pl.BlockSpec((g, N_HEADS * t, rope), lambda i: (i, 0, 0))]
        w_uq_ext, w_uk_t = q_operands
        q_in_specs = [_resident(w_uq_ext.shape, lambda i: (0, 0)), _resident(w_uk_t.shape, lambda i: (0, 0, 0))]
    else:
        tps = t // tm
        x_spec = pl.BlockSpec((1, tm, d), lambda i: (i // tps, i % tps, 0))
        mod_spec = lambda k: pl.BlockSpec((1, 1, d), lambda i: (i // tps, 0, k))
        tab_spec = pl.BlockSpec((tm, 128), lambda i: (i % tps, 0))
        hk = QK_NOPE_DIM + 128
        qk_shapes = [jax.ShapeDtypeStruct((N_HEADS, n_tok, hk), BF16),
                     jax.ShapeDtypeStruct((N_HEADS, V_HEAD_DIM + ONES_ROWS, n_tok), BF16),
                     jax.ShapeDtypeStruct((N_HEADS, hk, n_tok), BF16)]
        qk_specs = [pl.BlockSpec((N_HEADS, tm, hk), lambda i: (0, i, 0)),
                    pl.BlockSpec((N_HEADS, V_HEAD_DIM + ONES_ROWS, tm), lambda i: (0, 0, i)),
                    pl.BlockSpec((N_HEADS, hk, tm), lambda i: (0, 0, i))]
        cos_t, sin_t, w_uq_t, w_uk_all, w_uv_t_all = q_operands
        tab_t_spec = pl.BlockSpec((cos_t.shape[0], tm), lambda i: (0, i % tps))
        q_in_specs = [tab_t_spec, tab_t_spec, _resident(w_uq_t.shape, lambda i: (0, 0)),
                      _resident(w_uk_all.shape, lambda i: (0, 0)), _resident(w_uv_t_all.shape, lambda i: (0, 0))]
    vec = lambda n: pl.BlockSpec((1, n), lambda i: (0, 0))
    return pl.pallas_call(
        functools.partial(_mla_in_body, sample=sample),
        grid=(n_i,),
        in_specs=[
            x_spec, mod_spec(0), mod_spec(1), vec(d), vec(d),
            _resident(w_down_ext.shape, lambda i: (0, 0)),
            vec(lat), tab_spec, tab_spec,
            _resident(w_in.shape, lambda i: (0, 0)),
            vec(g_q.shape[0]),
        ] + q_in_specs,
        out_specs=[row_spec(lat), row_spec(rope), row_spec(d_attn)] + qk_specs,
        out_shape=[jax.ShapeDtypeStruct((n_tok, lat), F32), jax.ShapeDtypeStruct((n_tok, rope), F32),
                   jax.ShapeDtypeStruct((n_tok, d_attn), F32)] + qk_shapes,
        compiler_params=_params(("arbitrary",), 58),
        name="mla_in_sample" if sample else "mla_in_prompt",
    )(x, mod, mod, g_pre.reshape(1, d), kv_g_in.reshape(1, d), w_down_ext, kv_g_latent.reshape(1, lat),
      cos_tab, sin_tab, w_in, g_q.reshape(1, -1), *q_operands)


LANES = 128


def _softmax_step(s, v, m_scr, l_scr, acc_scr):
    n_chunks = s.shape[1] // LANES
    chunks = [s[:, c * LANES:(c + 1) * LANES] for c in range(n_chunks)]
    m_cur = chunks[0]
    for ch in chunks[1:]:
        m_cur = jnp.maximum(m_cur, ch)
    m_prev = m_scr[...]
    m_new = jnp.maximum(m_prev, jnp.max(m_cur, axis=1, keepdims=True))
    alpha = jnp.exp(m_prev - m_new)
    ps = [jnp.exp(ch - m_new) for ch in chunks]
    p_sum = ps[0]
    for p in ps[1:]:
        p_sum = p_sum + p
    l_scr[...] = alpha * l_scr[...] + p_sum
    m_scr[...] = m_new
    pv = _dot(jnp.concatenate(ps, axis=1).astype(BF16), v)
    for c in range(acc_scr.shape[1] // LANES):
        sl = slice(c * LANES, (c + 1) * LANES)
        acc_scr[:, sl] = alpha * acc_scr[:, sl] + pv[:, sl]


def _softmax_init(m_scr, l_scr, acc_scr):
    m_scr[...] = jnp.full(m_scr.shape, NEG_INF, F32)
    l_scr[...] = jnp.zeros(l_scr.shape, F32)
    acc_scr[...] = jnp.zeros(acc_scr.shape, F32)


def _softmax_inv_denominator(l_scr):
    return 1.0 / jnp.sum(l_scr[...], axis=1, keepdims=True)


def _attn_prompt_body(qt_ref, k_ref, vt_ref, z_ref, y_ref, m_scr, acc_scr, *, tq):
    t_seq = k_ref.shape[1]
    row = lax.broadcasted_iota(jnp.int32, (tq, tq), 0)
    col = lax.broadcasted_iota(jnp.int32, (tq, tq), 1)
    on_or_below_diagonal = row <= col

    def scores(qi, kj):
        s_t = _dot(k_ref[0, kj * tq:(kj + 1) * tq, :], qt_ref[0, :, qi * tq:(qi + 1) * tq])
        return jnp.where(on_or_below_diagonal, s_t, NEG_INF) if kj == qi else s_t

    pairs = [(qi, kj) for qi in range(t_seq // tq) for kj in range(qi + 1)]
    s_next = scores(*pairs[0])
    for idx, (qi, kj) in enumerate(pairs):
        s_t = s_next
        if idx + 1 < len(pairs):
            s_next = scores(*pairs[idx + 1])
        qs = slice(qi * tq, (qi + 1) * tq)
        ks = slice(kj * tq, (kj + 1) * tq)
        m_cur = jnp.max(s_t, axis=0, keepdims=True)
        if kj == 0:
            m_new = m_cur
        else:
            m_prev = m_scr[...]
            m_new = jnp.maximum(m_prev, m_cur)
            alpha = jnp.exp2(m_prev - m_new)
        p_t = jnp.exp2(s_t - m_new).astype(BF16)
        pv_t = _dot(vt_ref[0, :, ks], p_t)
        if kj == 0:
            acc_scr[...] = pv_t
        else:
            acc_scr[...] = alpha * acc_scr[...] + pv_t
        m_scr[...] = m_new
        if kj == qi:
            vd = y_ref.shape[1]
            o = (acc_scr[:vd, :] * (1.0 / acc_scr[vd:vd + 1, :])).T
            y_ref[qs, :] = (o * _silu(z_ref[qs, :])).astype(BF16)


def _attn_prompt(q_t, k_heads, v_t, z, *, nseq):
    nh, hk, n_tok = q_t.shape
    v_rows = v_t.shape[1]
    vd = v_rows - ONES_ROWS
    t = n_tok // nseq
    tq = 512
    return pl.pallas_call(
        functools.partial(_attn_prompt_body, tq=tq),
        grid=(nseq, nh),
        in_specs=[
            pl.BlockSpec((1, hk, t), lambda b, h: (h, 0, b)),
            pl.BlockSpec((1, t, hk), lambda b, h: (h, b, 0)),
            pl.BlockSpec((1, v_rows, t), lambda b, h: (h, 0, b)),
            pl.BlockSpec((t, vd), lambda b, h: (b, h)),
        ],
        out_specs=pl.BlockSpec((t, vd), lambda b, h: (b, h)),
        out_shape=jax.ShapeDtypeStruct((n_tok, nh * vd), BF16),
        scratch_shapes=[pltpu.VMEM((1, tq), F32), pltpu.VMEM((v_rows, tq), F32)],
        compiler_params=_params(("arbitrary", "arbitrary"), 48),
        name="attn_prompt",
    )(q_t, k_heads, v_t, z)


def _attn_sample_body(pt_ref, q_ref, qp_ref, ckvn_ref, kpen_ref, z_ref, wuv_ref, ck_hbm, kpt_hbm, y_ref,
                      m_scr, l_scr, acc_scr, k_buf, kpt_buf, sem, *, n_pg, n_chunks):
    b = pl.program_id(0)
    page = ck_hbm.shape[1]
    rows, lat = acc_scr.shape
    t_new = ckvn_ref.shape[0]
    n_slots = k_buf.shape[0]
    ahead = n_slots - 1
    assert n_chunks % n_slots == 0 and ahead <= n_chunks

    def chunk_copies(seq, c):
        slot = c % n_slots
        copies = []
        for k in range(n_pg):
            pg = pt_ref[seq, c * n_pg + k]
            copies.append(pltpu.make_async_copy(
                ck_hbm.at[pg], k_buf.at[slot, pl.ds(k * page, page), :], sem.at[0, slot]))
            copies.append(pltpu.make_async_copy(kpt_hbm.at[pg], kpt_buf.at[slot, k], sem.at[1, slot]))
        return copies

    def start_chunk(seq, c):
        for i, cp in enumerate(chunk_copies(seq, c)):
            cp.start(priority=(i // 2 + i % 2) % 2)

    @pl.when(b == 0)
    def _():
        for c in range(ahead):
            start_chunk(b, c)

    _softmax_init(m_scr, l_scr, acc_scr)
    q = q_ref[0].astype(BF16)
    qp = qp_ref[0].astype(BF16)

    def wait_and_score(c):
        for cp in chunk_copies(b, c):
            cp.wait()
        slot = c % n_slots
        kk = k_buf[slot].astype(BF16)
        kpt = jnp.concatenate([kpt_buf[slot, k] for k in range(n_pg)], axis=1).astype(BF16)
        return _dot_nt(q, kk) + _dot(qp, kpt), kk

    scored_next = wait_and_score(0)
    for c in range(n_chunks):
        nxt = c + ahead
        if nxt < n_chunks:
            start_chunk(b, nxt)
        else:
            @pl.when(b + 1 < pl.num_programs(0))
            def _():
                start_chunk(b + 1, nxt - n_chunks)
        s, kk = scored_next
        if c + 1 < n_chunks:
            scored_next = wait_and_score(c + 1)
        _softmax_step(s, kk, m_scr, l_scr, acc_scr)

    pad = page - t_new
    kn = jnp.concatenate([ckvn_ref[...], jnp.zeros((pad, lat), F32)], axis=0).astype(BF16)
    kpn = jnp.concatenate([kpen_ref[...], jnp.zeros((pad, kpen_ref.shape[1]), F32)], axis=0).astype(BF16)
    s = _dot_nt(q, kn) + _dot_nt(qp, kpn)
    t_q = lax.broadcasted_iota(jnp.int32, (rows, page), 0) & (t_new - 1)
    t_k = lax.broadcasted_iota(jnp.int32, (rows, page), 1)
    _softmax_step(jnp.where(t_k <= t_q, s, NEG_INF), kn, m_scr, l_scr, acc_scr)

    ol = (acc_scr[...] * _softmax_inv_denominator(l_scr)).astype(BF16)
    o_all = _dot(ol, wuv_ref[...])
    vd = wuv_ref.shape[1] // N_HEADS
    for hd in range(N_HEADS):
        o = o_all[hd * t_new:(hd + 1) * t_new, hd * vd:(hd + 1) * vd]
        y_ref[:, hd * vd:(hd + 1) * vd] = o * _silu(z_ref[:, hd * vd:(hd + 1) * vd])


def _attn_sample(qlat, qpe, cache_ckv, cache_kpe_t, page_table, ckv_new, kpe_new, z, w_uv_all):
    nseq, rows, lat = qlat.shape
    rope = qpe.shape[2]
    page = cache_ckv.shape[1]
    n_pages = page_table.shape[1]
    t_new = ckv_new.shape[0] // nseq
    d_attn = z.shape[1]
    n_pg = 16
    n_chunks = n_pages // n_pg
    n_slots = 4

    grid_spec = pltpu.PrefetchScalarGridSpec(
        num_scalar_prefetch=1,
        grid=(nseq,),
        in_specs=[
            pl.BlockSpec((1, rows, lat), lambda b, pt: (b, 0, 0)),
            pl.BlockSpec((1, rows, rope), lambda b, pt: (b, 0, 0)),
            pl.BlockSpec((t_new, lat), lambda b, pt: (b, 0)),
            pl.BlockSpec((t_new, rope), lambda b, pt: (b, 0)),
            pl.BlockSpec((t_new, d_attn), lambda b, pt: (b, 0)),
            _resident(w_uv_all.shape, lambda b, pt: (0, 0)),
            pl.BlockSpec(memory_space=pl.ANY),
            pl.BlockSpec(memory_space=pl.ANY),
        ],
        out_specs=pl.BlockSpec((t_new, d_attn), lambda b, pt: (b, 0)),
        scratch_shapes=[pltpu.VMEM((rows, LANES), F32), pltpu.VMEM((rows, LANES), F32),
                        pltpu.VMEM((rows, lat), F32),
                        pltpu.VMEM((n_slots, n_pg * page, lat), F32),
                        pltpu.VMEM((n_slots, n_pg, rope, page), F32),
                        pltpu.SemaphoreType.DMA((2, n_slots))],
    )
    return pl.pallas_call(
        functools.partial(_attn_sample_body, n_pg=n_pg, n_chunks=n_chunks),
        grid_spec=grid_spec,
        out_shape=jax.ShapeDtypeStruct((nseq * t_new, d_attn), F32),
        compiler_params=_params(("arbitrary",), 48),
        name="attn_sample",
    )(page_table, qlat, qpe, ckv_new, kpe_new, z, w_uv_all, cache_ckv, cache_kpe_t)


def _rope_tables(pos):
    half = QK_ROPE_DIM // 2
    lane = jnp.arange(2 * QK_ROPE_DIM)
    inv_freq = ROPE_THETA ** (-(lane % half).astype(F32) / half)
    ang = pos.astype(F32)[:, None] * inv_freq[None, :]
    sign = jnp.where((lane // half) % 2 == 0, -1.0, 1.0).astype(F32)
    return jnp.cos(ang), jnp.sin(ang) * sign[None, :]


def _rope_tables_transposed(pos):
    half = QK_ROPE_DIM // 2
    inv_freq = ROPE_THETA ** (-jnp.arange(half, dtype=F32) / half)
    ang = inv_freq[:, None] * pos.astype(F32)[None, :]
    return jnp.cos(ang), jnp.sin(ang)


def _swap_halves(w, half):
    return jnp.concatenate([w[..., half:], w[..., :half]], axis=-1)


def kernel(x_prompt, x_sample, state_conv, cache_ckv, cache_kpe, page_table, c_prompt, c_sample, w_ada, b_ada, g_pre, g_post, conv_w_in, conv_w, conv_w_out, kv_g_in, kv_w_down, kv_g_latent, kv_w_up, mla_w_in, mla_g_q, mla_w_uq, mla_w_out):
    nb, seq, d = x_prompt.shape
    nd, t_dec, _ = x_sample.shape
    depth = w_ada.shape[0]
    n_a = conv_w_in.shape[0]
    assert depth == 2 and n_a == 1 and mla_w_in.shape[0] == 1
    lat = kv_g_latent.shape[0]
    rope = QK_ROPE_DIM
    half = rope // 2
    past_len = page_table.shape[1] * cache_ckv.shape[1]

    n_c = nb + nd
    n_c_pad = -(-n_c // 8) * 8
    c_all = jnp.concatenate([c_prompt, c_sample, jnp.zeros((n_c_pad - n_c, d), F32)], axis=0)
    mod = _ada(c_all, w_ada, b_ada)
    mod_p = [mod[l, :nb].reshape(nb, 1, 3 * d) for l in range(depth)]
    mod_s = [mod[l, nb:n_c].reshape(nd, 1, 3 * d) for l in range(depth)]

    w_in0 = conv_w_in[0].astype(BF16)
    w_out0 = conv_w_out[0].astype(BF16)
    y_p, conv_p = _conv_in(x_prompt, mod_p[0], g_pre[0], None, w_in0, conv_w[0], sample=False)
    xp = _out_proj(y_p, x_prompt, mod_p[0], g_post[0], w_out0, name="out_proj_conv_prompt")
    y_s, conv_s = _conv_in(x_sample, mod_s[0], g_pre[0], state_conv[0], w_in0, conv_w[0], sample=True)
    xs = _out_proj(y_s, x_sample, mod_s[0], g_post[0], w_out0, name="out_proj_conv_sample")

    w_kpe = kv_w_down[:, lat:]
    w_kpe_sw = _swap_halves(w_kpe, half)
    w_down_ext = jnp.concatenate([kv_w_down[:, :lat], w_kpe, w_kpe, w_kpe_sw, w_kpe_sw], axis=1).astype(BF16)
    qr = mla_g_q.shape[1]
    w_uq = mla_w_uq[0].reshape(qr, N_HEADS, QK_NOPE_DIM + rope)
    w_uq_pe = w_uq[:, :, QK_NOPE_DIM:]
    w_uq_ext = jnp.concatenate(
        [w_uq[:, :, :QK_NOPE_DIM].reshape(qr, -1), w_uq_pe.reshape(qr, -1),
         _swap_halves(w_uq_pe, half).reshape(qr, -1)], axis=1).astype(BF16)
    w_uk_t = jnp.transpose(kv_w_up[:, :, :QK_NOPE_DIM], (1, 2, 0)).astype(BF16)
    w_uk_all = kv_w_up[:, :, :QK_NOPE_DIM].reshape(lat, -1).astype(BF16)
    w_uv_all = kv_w_up[:, :, QK_NOPE_DIM:].reshape(lat, -1).astype(BF16)
    w_uv_t_all = jnp.transpose(kv_w_up[:, :, QK_NOPE_DIM:], (1, 2, 0)).reshape(-1, lat).astype(BF16)
    w_uq_t = mla_w_uq[0].T.astype(BF16)
    w_in1 = mla_w_in[0].astype(BF16)
    w_out1 = mla_w_out[0].astype(BF16)

    cos_p, sin_p = _rope_tables(jnp.arange(seq))
    cos_s, sin_s = _rope_tables(past_len + jnp.arange(t_dec))
    reps = 256 // t_dec
    cos_s, sin_s = jnp.tile(cos_s, (reps, 1)), jnp.tile(sin_s, (reps, 1))
    cos_t, sin_t = _rope_tables_transposed(jnp.arange(seq))

    ckv_p, kpe_p, z_p, k_heads, v_t, q_t = _mla_in(
        xp, mod_p[1], g_pre[1], kv_g_in, w_down_ext, kv_g_latent, cos_p, sin_p, w_in1, mla_g_q[0],
        (cos_t, sin_t, w_uq_t, w_uk_all, w_uv_t_all), sample=False)
    y_p = _attn_prompt(q_t, k_heads, v_t, z_p, nseq=nb)
    xp = _out_proj(y_p, xp, mod_p[1], g_post[1], w_out1, name="out_proj_mla_prompt")

    ckv_s, kpe_s, z_s, qlat_s, qpe_s = _mla_in(
        xs, mod_s[1], g_pre[1], kv_g_in, w_down_ext, kv_g_latent, cos_s, sin_s, w_in1, mla_g_q[0],
        (w_uq_ext, w_uk_t), sample=True)
    cache_kpe_t = jnp.swapaxes(cache_kpe, 1, 2)
    y_s = _attn_sample(qlat_s, qpe_s, cache_ckv, cache_kpe_t, page_table, ckv_s, kpe_s, z_s, w_uv_all)
    xs = _out_proj(y_s, xs, mod_s[1], g_post[1], w_out1, name="out_proj_mla_sample")

    return (xp, xs, conv_p[None], conv_s[None],
            ckv_p.reshape(nb, seq, lat), kpe_p.reshape(nb, seq, rope),
            ckv_s.reshape(nd, t_dec, lat), kpe_s.reshape(nd, t_dec, rope))
```

```python
import functools

import jax
import jax.numpy as jnp
from jax import lax
from jax.experimental import pallas as pl
from jax.experimental.pallas import tpu as pltpu

F32 = jnp.float32
BF16 = jnp.bfloat16
RMS_EPS = 1e-6
ROPE_THETA = 10000.0
MIB = 1024 * 1024
NEG_INF = float("-inf")

N_HEADS = 16
QK_NOPE_DIM = 128
QK_ROPE_DIM = 64
V_HEAD_DIM = 128
SOFTMAX_SCALE = (QK_NOPE_DIM + QK_ROPE_DIM) ** -0.5
LOG2_E = 1.4426950408889634
ONES_ROWS = 16


def _dot(a, b):
    return jnp.dot(a, b, preferred_element_type=F32)


def _dot_nt(a, b):
    return lax.dot_general(a, b, (((1,), (1,)), ((), ())), preferred_element_type=F32)


def _silu(x):
    return x * jax.nn.sigmoid(x)


def _unit_rms(x):
    return x * lax.rsqrt(jnp.mean(x * x, axis=-1, keepdims=True) + RMS_EPS)


def _params(semantics, vmem_mib):
    return pltpu.CompilerParams(dimension_semantics=semantics, vmem_limit_bytes=vmem_mib * MIB)


def _resident(block_shape, index_map):
    return pl.BlockSpec(block_shape, index_map, pipeline_mode=pl.Buffered(1))


def _ada_body(c_ref, w_ref, b_ref, o_ref):
    a = _silu(c_ref[...]).astype(BF16)
    o_ref[0] = _dot(a, w_ref[0].astype(BF16)) + b_ref[0]


def _ada(c_all, w_ada, b_ada):
    depth, d, d3 = w_ada.shape
    m = c_all.shape[0]
    tn = 1024
    return pl.pallas_call(
        _ada_body,
        grid=(depth, d3 // tn),
        in_specs=[
            pl.BlockSpec((m, d), lambda l, j: (0, 0)),
            pl.BlockSpec((1, d, tn), lambda l, j: (l, 0, j)),
            pl.BlockSpec((1, 1, tn), lambda l, j: (l, 0, j)),
        ],
        out_specs=pl.BlockSpec((1, m, tn), lambda l, j: (l, 0, j)),
        out_shape=jax.ShapeDtypeStruct((depth, m, d3), F32),
        compiler_params=_params(("arbitrary", "arbitrary"), 40),
        name="ada",
    )(c_all, w_ada, b_ada.reshape(depth, 1, d3))


CONV_COL_SPLIT = 2


def _conv_in_body(*refs, sample, tiles_per_seq):
    if sample:
        (x_ref, shift_ref, scale_ref, gpre_ref, st_ref, wb_ref, wc_ref, wv_ref, wz_ref, wconv_ref,
         y_ref, cs_ref, h_scr) = refs
    else:
        (x_ref, shift_ref, scale_ref, gpre_ref, wb_ref, wc_ref, wv_ref, wz_ref, wconv_ref,
         y_ref, cs_ref, h_scr, carry_scr) = refs
    i = pl.program_id(0)
    j = pl.program_id(1)
    g, r, d = x_ref.shape
    rows = g * r

    @pl.when(j == 0)
    def _():
        xn = _unit_rms(x_ref[...])
        h = xn * gpre_ref[...][None] * (1.0 + scale_ref[...]) + shift_ref[...]
        h_scr[...] = h.reshape(rows, d).astype(BF16)

    if not sample:
        @pl.when(i % tiles_per_seq == 0)
        def _():
            carry_scr[j] = jnp.zeros(carry_scr.shape[1:], F32)

    h = h_scr[...]
    tn = y_ref.shape[1]
    tc = tn // CONV_COL_SPLIT
    for cb in range(CONV_COL_SPLIT):
        cols = slice(cb * tc, (cb + 1) * tc)
        bg = _dot(h, wb_ref[:, cols])
        cg = _dot(h, wc_ref[:, cols])
        v = _dot(h, wv_ref[:, cols])
        z = _dot(h, wz_ref[:, cols])
        u = cg * v
        t = lax.broadcasted_iota(jnp.int32, (rows, tc), 0)
        if sample:
            t = t & (r - 1)
            st = st_ref[:, :, cols]
            s0 = jnp.broadcast_to(st[:, 0:1, :], (g, r, tc)).reshape(rows, tc)
            s1 = jnp.broadcast_to(st[:, 1:2, :], (g, r, tc)).reshape(rows, tc)
        else:
            carry = carry_scr[j, :, cols]
            s0 = carry[6:7, :]
            s1 = carry[7:8, :]
        p1 = jnp.where(t == 0, s1, pltpu.roll(u, 1, 0))
        p2 = jnp.where(t == 0, s0, jnp.where(t == 1, s1, pltpu.roll(u, 2, 0)))
        w = wconv_ref[:, cols]
        conv = w[0:1, :] * p2 + w[1:2, :] * p1 + w[2:3, :] * u
        y_ref[:, cols] = (bg * conv * _silu(z)).astype(BF16)
        if sample:
            cs_ref[:, :, cols] = u.reshape(g, r, tc)[:, r - 2:r, :]
        else:
            tail = u[rows - 8:rows, :]
            carry_scr[j, :, cols] = tail
            cs_ref[0, :, cols] = tail


def _conv_in(x, mod, g_pre, state, w_in, w_conv, *, sample):
    nseq, t, d = x.shape
    dc = w_conv.shape[1]
    tn = 512
    nj = dc // tn
    tm = 512
    if sample:
        g, r = tm // t, t
        n_i = nseq // g
        tps = 1
        x_spec = pl.BlockSpec((g, r, d), lambda i, j: (i, 0, 0))
        mod_spec = lambda k: pl.BlockSpec((g, 1, d), lambda i, j: (i, 0, k))
    else:
        g, r = 1, tm
        tps = t // tm
        n_i = nseq * tps
        x_spec = pl.BlockSpec((1, tm, d), lambda i, j: (i // tps, i % tps, 0))
        mod_spec = lambda k: pl.BlockSpec((1, 1, d), lambda i, j: (i // tps, 0, k))
    w_spec = lambda s: pl.BlockSpec((d, tn), lambda i, j: (0, s * nj + j))
    in_specs = [x_spec, mod_spec(0), mod_spec(1), pl.BlockSpec((1, d), lambda i, j: (0, 0))]
    args = [x, mod, mod, g_pre.reshape(1, d)]
    scratch = [pltpu.VMEM((g * r, d), BF16)]
    if sample:
        in_specs.append(pl.BlockSpec((g, 2, tn), lambda i, j: (i, 0, j)))
        args.append(state)
        cs_shape = jax.ShapeDtypeStruct((nseq, 2, dc), F32)
        cs_spec = pl.BlockSpec((g, 2, tn), lambda i, j: (i, 0, j))
    else:
        cs_shape = jax.ShapeDtypeStruct((n_i, 8, dc), F32)
        cs_spec = pl.BlockSpec((1, 8, tn), lambda i, j: (i, 0, j))
        scratch.append(pltpu.VMEM((nj, 8, tn), F32))
    in_specs += [w_spec(0), w_spec(1), w_spec(2), w_spec(3), pl.BlockSpec((3, tn), lambda i, j: (0, j))]
    args += [w_in, w_in, w_in, w_in, w_conv]
    y, cs = pl.pallas_call(
        functools.partial(_conv_in_body, sample=sample, tiles_per_seq=tps),
        grid=(n_i, nj),
        in_specs=in_specs,
        out_specs=[pl.BlockSpec((g * r, tn), lambda i, j: (i, j)), cs_spec],
        out_shape=[jax.ShapeDtypeStruct((nseq * t, dc), BF16), cs_shape],
        scratch_shapes=scratch,
        compiler_params=_params(("arbitrary", "arbitrary"), 56),
        name="conv_in_sample" if sample else "conv_in_prompt",
    )(*args)
    if not sample:
        cs = cs.reshape(nseq, tps, 8, dc)[:, tps - 1, 6:8, :]
    return y, cs


def _out_proj_body(y_ref, x_ref, gate_ref, gpost_ref, w_ref, o_ref):
    g, r, d = x_ref.shape
    o = _dot(y_ref[...].astype(BF16), w_ref[...])
    o = _unit_rms(o) * gpost_ref[...]
    o_ref[...] = x_ref[...] + gate_ref[...] * o.reshape(g, r, d)


def _out_proj(y, x, mod, g_post, w_out, *, name):
    nseq, t, d = x.shape
    tm = 512
    if t >= tm:
        g, r = 1, tm
        tps = t // tm
        x_spec = pl.BlockSpec((1, tm, d), lambda i: (i // tps, i % tps, 0))
        gate_spec = pl.BlockSpec((1, 1, d), lambda i: (i // tps, 0, 2))
    else:
        g, r = tm // t, t
        tps = 1
        x_spec = pl.BlockSpec((g, r, d), lambda i: (i, 0, 0))
        gate_spec = pl.BlockSpec((g, 1, d), lambda i: (i, 0, 2))
    n_i = nseq * t // tm
    return pl.pallas_call(
        _out_proj_body,
        grid=(n_i,),
        in_specs=[
            pl.BlockSpec((tm, y.shape[1]), lambda i: (i, 0)),
            x_spec,
            gate_spec,
            pl.BlockSpec((1, d), lambda i: (0, 0)),
            _resident(w_out.shape, lambda i: (0, 0)),
        ],
        out_specs=x_spec,
        out_shape=jax.ShapeDtypeStruct(x.shape, F32),
        compiler_params=_params(("arbitrary",), 48),
        name=name,
    )(y, x, mod, g_post.reshape(1, d), w_out)


MLA_TILE_ROWS = 256


def _mla_in_body(x_ref, shift_ref, scale_ref, gpre_ref, gkv_ref, wdown_ref, glat_ref, cos_ref, sin_ref,
                 win_ref, gq_ref, *refs, sample):
    if sample:
        wuq_ref, wuk_ref, ckv_ref, kpe_ref, z_ref, qlat_ref, qpe_ref = refs
    else:
        cost_ref, sint_ref, wuqt_ref, wuk_ref, wuvt_ref, ckv_ref, kpe_ref, z_ref, kh_ref, vt_ref, qt_ref = refs
    g, r, d = x_ref.shape
    rows = g * r
    lat = glat_ref.shape[1]
    rope = QK_ROPE_DIM
    xn = _unit_rms(x_ref[...])
    cos = cos_ref[...]
    sin = sin_ref[...]

    a = (xn * gkv_ref[...][None]).reshape(rows, d).astype(BF16)
    raw = _dot(a, wdown_ref[...])
    ckv = _unit_rms(raw[:, :lat]) * glat_ref[...]
    kpe2 = raw[:, lat:lat + 128] * cos + raw[:, lat + 128:lat + 256] * sin
    ckv_ref[...] = ckv
    kpe_ref[...] = kpe2[:, :rope]
    nope = QK_NOPE_DIM
    lane = lax.broadcasted_iota(jnp.int32, (rows, 128), 1)
    if not sample:
        ckv_b = ckv.astype(BF16)
        k_nope = _dot(ckv_b, wuk_ref[...])
        v_t = _dot_nt(wuvt_ref[...], ckv_b)
        kpe_pad = jnp.where(lane < rope, kpe2, 0.0).astype(BF16)
        vd = vt_ref.shape[1] - ONES_ROWS
        for hd in range(N_HEADS):
            kh_ref[hd, :, :nope] = k_nope[:, hd * nope:(hd + 1) * nope].astype(BF16)
            kh_ref[hd, :, nope:] = kpe_pad
            vt_ref[hd, :vd, :] = v_t[hd * vd:(hd + 1) * vd, :].astype(BF16)
            vt_ref[hd, vd:, :] = jnp.ones((ONES_ROWS, rows), BF16)

    h = xn * gpre_ref[...][None] * (1.0 + scale_ref[...]) + shift_ref[...]
    qz = _dot(h.reshape(rows, d).astype(BF16), win_ref[...])
    qr = gq_ref.shape[1]
    z_ref[...] = qz[:, qr:]
    qa = (_unit_rms(qz[:, :qr]) * gq_ref[...]).astype(BF16)
    if not sample:
        q_t = _dot_nt(wuqt_ref[...], qa) * (SOFTMAX_SCALE * LOG2_E)
        cos_t = cost_ref[...]
        sin_t = sint_ref[...]
        hq = nope + rope
        half = rope // 2
        for hd in range(N_HEADS):
            base = hd * hq
            x1 = q_t[base + nope:base + nope + half, :]
            x2 = q_t[base + nope + half:base + hq, :]
            qt_ref[hd, :nope, :] = q_t[base:base + nope, :].astype(BF16)
            qt_ref[hd, nope:nope + half, :] = (x1 * cos_t - x2 * sin_t).astype(BF16)
            qt_ref[hd, nope + half:hq, :] = (x2 * cos_t + x1 * sin_t).astype(BF16)
            qt_ref[hd, hq:, :] = jnp.zeros((qt_ref.shape[1] - hq, rows), BF16)
        return

    q = _dot(qa, wuq_ref[...]) * SOFTMAX_SCALE
    n_nope = N_HEADS * nope
    n_rope = N_HEADS * rope
    for hd in range(N_HEADS):
        qn = q[:, hd * nope:(hd + 1) * nope].astype(BF16)
        ql = _dot(qn, wuk_ref[hd])
        qlat_ref[:, hd * r:(hd + 1) * r, :] = ql.reshape(g, r, lat)
    for pr in range(N_HEADS // 2):
        lo = n_nope + pr * 128
        pe = q[:, lo:lo + 128] * cos + q[:, lo + n_rope:lo + n_rope + 128] * sin
        for k in range(2):
            hd = 2 * pr + k
            qpe_ref[:, hd * r:(hd + 1) * r, :] = pe[:, k * rope:(k + 1) * rope].reshape(g, r, rope)


def _mla_in(x, mod, g_pre, kv_g_in, w_down_ext, kv_g_latent, cos_tab, sin_tab, w_in, g_q, q_operands, *, sample):
    nseq, t, d = x.shape
    n_tok = nseq * t
    lat = kv_g_latent.shape[0]
    rope = QK_ROPE_DIM
    d_attn = w_in.shape[1] - g_q.shape[0]
    tm = MLA_TILE_ROWS
    n_i = n_tok // tm
    row_spec = lambda w: pl.BlockSpec((tm, w), lambda i: (i, 0))
    if sample:
        g, r = tm // t, t
        x_spec = pl.BlockSpec((g, r, d), lambda i: (i, 0, 0))
        mod_spec = lambda k: pl.BlockSpec((g, 1, d), lambda i: (i, 0, k))
        tab_spec = pl.BlockSpec((tm, 128), lambda i: (0, 0))
        qk_shapes = [jax.ShapeDtypeStruct((nseq, N_HEADS * t, lat), F32),
                     jax.ShapeDtypeStruct((nseq, N_HEADS * t, rope), F32)]
        qk_specs = [pl.BlockSpec((g, N_HEADS * t, lat), lambda i: (i, 0, 0)),
                    pl.BlockSpec((g, N_HEADS * t, rope), lambda i: (i, 0, 0))]
        w_uq_ext, w_uk_t = q_operands
        q_in_specs = [_resident(w_uq_ext.shape, lambda i: (0, 0)), _resident(w_uk_t.shape, lambda i: (0, 0, 0))]
    else:
        tps = t // tm
        x_spec = pl.BlockSpec((1, tm, d), lambda i: (i // tps, i % tps, 0))
        mod_spec = lambda k: pl.BlockSpec((1, 1, d), lambda i: (i // tps, 0, k))
        tab_spec = pl.BlockSpec((tm, 128), lambda i: (i % tps, 0))
        hk = QK_NOPE_DIM + 128
        qk_shapes = [jax.ShapeDtypeStruct((N_HEADS, n_tok, hk), BF16),
                     jax.ShapeDtypeStruct((N_HEADS, V_HEAD_DIM + ONES_ROWS, n_tok), BF16),
                     jax.ShapeDtypeStruct((N_HEADS, hk, n_tok), BF16)]
        qk_specs = [pl.BlockSpec((N_HEADS, tm, hk), lambda i: (0, i, 0)),
                    pl.BlockSpec((N_HEADS, V_HEAD_DIM + ONES_ROWS, tm), lambda i: (0, 0, i)),
                    pl.BlockSpec((N_HEADS, hk, tm), lambda i: (0, 0, i))]
        cos_t, sin_t, w_uq_t, w_uk_all, w_uv_t_all = q_operands
        tab_t_spec = pl.BlockSpec((cos_t.shape[0], tm), lambda i: (0, i % tps))
        q_in_specs = [tab_t_spec, tab_t_spec, _resident(w_uq_t.shape, lambda i: (0, 0)),
                      _resident(w_uk_all.shape, lambda i: (0, 0)), _resident(w_uv_t_all.shape, lambda i: (0, 0))]
    vec = lambda n: pl.BlockSpec((1, n), lambda i: (0, 0))
    return pl.pallas_call(
        functools.partial(_mla_in_body, sample=sample),
        grid=(n_i,),
        in_specs=[
            x_spec, mod_spec(0), mod_spec(1), vec(d), vec(d),
            _resident(w_down_ext.shape, lambda i: (0, 0)),
            vec(lat), tab_spec, tab_spec,
            _resident(w_in.shape, lambda i: (0, 0)),
            vec(g_q.shape[0]),
        ] + q_in_specs,
        out_specs=[row_spec(lat), row_spec(rope), row_spec(d_attn)] + qk_specs,
        out_shape=[jax.ShapeDtypeStruct((n_tok, lat), F32), jax.ShapeDtypeStruct((n_tok, rope), F32),
                   jax.ShapeDtypeStruct((n_tok, d_attn), F32)] + qk_shapes,
        compiler_params=_params(("arbitrary",), 58),
        name="mla_in_sample" if sample else "mla_in_prompt",
    )(x, mod, mod, g_pre.reshape(1, d), kv_g_in.reshape(1, d), w_down_ext, kv_g_latent.reshape(1, lat),
      cos_tab, sin_tab, w_in, g_q.reshape(1, -1), *q_operands)


LANES = 128


def _softmax_step(s, v, m_scr, l_scr, acc_scr):
    n_chunks = s.shape[1] // LANES
    chunks = [s[:, c * LANES:(c + 1) * LANES] for c in range(n_chunks)]
    m_cur = chunks[0]
    for ch in chunks[1:]:
        m_cur = jnp.maximum(m_cur, ch)
    m_prev = m_scr[...]
    m_new = jnp.maximum(m_prev, jnp.max(m_cur, axis=1, keepdims=True))
    alpha = jnp.exp(m_prev - m_new)
    ps = [jnp.exp(ch - m_new) for ch in chunks]
    p_sum = ps[0]
    for p in ps[1:]:
        p_sum = p_sum + p
    l_scr[...] = alpha * l_scr[...] + p_sum
    m_scr[...] = m_new
    pv = _dot(jnp.concatenate(ps, axis=1).astype(BF16), v)
    for c in range(acc_scr.shape[1] // LANES):
        sl = slice(c * LANES, (c + 1) * LANES)
        acc_scr[:, sl] = alpha * acc_scr[:, sl] + pv[:, sl]


def _softmax_init(m_scr, l_scr, acc_scr):
    m_scr[...] = jnp.full(m_scr.shape, NEG_INF, F32)
    l_scr[...] = jnp.zeros(l_scr.shape, F32)
    acc_scr[...] = jnp.zeros(acc_scr.shape, F32)


def _softmax_inv_denominator(l_scr):
    return 1.0 / jnp.sum(l_scr[...], axis=1, keepdims=True)


def _attn_prompt_body(qt_ref, k_ref, vt_ref, z_ref, y_ref, m_scr, acc_scr, *, tq):
    t_seq = k_ref.shape[1]
    row = lax.broadcasted_iota(jnp.int32, (tq, tq), 0)
    col = lax.broadcasted_iota(jnp.int32, (tq, tq), 1)
    on_or_below_diagonal = row <= col

    def scores(qi, kj):
        s_t = _dot(k_ref[0, kj * tq:(kj + 1) * tq, :], qt_ref[0, :, qi * tq:(qi + 1) * tq])
        return jnp.where(on_or_below_diagonal, s_t, NEG_INF) if kj == qi else s_t

    pairs = [(qi, kj) for qi in range(t_seq // tq) for kj in range(qi + 1)]
    s_next = scores(*pairs[0])
    for idx, (qi, kj) in enumerate(pairs):
        s_t = s_next
        if idx + 1 < len(pairs):
            s_next = scores(*pairs[idx + 1])
        qs = slice(qi * tq, (qi + 1) * tq)
        ks = slice(kj * tq, (kj + 1) * tq)
        m_cur = jnp.max(s_t, axis=0, keepdims=True)
        if kj == 0:
            m_new = m_cur
        else:
            m_prev = m_scr[...]
            m_new = jnp.maximum(m_prev, m_cur)
            alpha = jnp.exp2(m_prev - m_new)
        p_t = jnp.exp2(s_t - m_new).astype(BF16)
        pv_t = _dot(vt_ref[0, :, ks], p_t)
        if kj == 0:
            acc_scr[...] = pv_t
        else:
            acc_scr[...] = alpha * acc_scr[...] + pv_t
        m_scr[...] = m_new
        if kj == qi:
            vd = y_ref.shape[1]
            o = (acc_scr[:vd, :] * (1.0 / acc_scr[vd:vd + 1, :])).T
            y_ref[qs, :] = (o * _silu(z_ref[qs, :])).astype(BF16)


def _attn_prompt(q_t, k_heads, v_t, z, *, nseq):
    nh, hk, n_tok = q_t.shape
    v_rows = v_t.shape[1]
    vd = v_rows - ONES_ROWS
    t = n_tok // nseq
    tq = 512
    return pl.pallas_call(
        functools.partial(_attn_prompt_body, tq=tq),
        grid=(nseq, nh),
        in_specs=[
            pl.BlockSpec((1, hk, t), lambda b, h: (h, 0, b)),
            pl.BlockSpec((1, t, hk), lambda b, h: (h, b, 0)),
            pl.BlockSpec((1, v_rows, t), lambda b, h: (h, 0, b)),
            pl.BlockSpec((t, vd), lambda b, h: (b, h)),
        ],
        out_specs=pl.BlockSpec((t, vd), lambda b, h: (b, h)),
        out_shape=jax.ShapeDtypeStruct((n_tok, nh * vd), BF16),
        scratch_shapes=[pltpu.VMEM((1, tq), F32), pltpu.VMEM((v_rows, tq), F32)],
        compiler_params=_params(("arbitrary", "arbitrary"), 48),
        name="attn_prompt",
    )(q_t, k_heads, v_t, z)


def _attn_sample_body(pt_ref, q_ref, qp_ref, ckvn_ref, kpen_ref, z_ref, wuv_ref, ck_hbm, kpt_hbm, y_ref,
                      m_scr, l_scr, acc_scr, k_buf, kpt_buf, sem, *, n_pg, n_chunks):
    b = pl.program_id(0)
    page = ck_hbm.shape[1]
    rows, lat = acc_scr.shape
    t_new = ckvn_ref.shape[0]
    n_slots = k_buf.shape[0]
    ahead = n_slots - 1
    assert n_chunks % n_slots == 0 and ahead <= n_chunks

    def chunk_copies(seq, c):
        slot = c % n_slots
        copies = []
        for k in range(n_pg):
            pg = pt_ref[seq, c * n_pg + k]
            copies.append(pltpu.make_async_copy(
                ck_hbm.at[pg], k_buf.at[slot, pl.ds(k * page, page), :], sem.at[0, slot]))
            copies.append(pltpu.make_async_copy(kpt_hbm.at[pg], kpt_buf.at[slot, k], sem.at[1, slot]))
        return copies

    def start_chunk(seq, c):
        for cp in chunk_copies(seq, c):
            cp.start()

    @pl.when(b == 0)
    def _():
        for c in range(ahead):
            start_chunk(b, c)

    _softmax_init(m_scr, l_scr, acc_scr)
    q = q_ref[0].astype(BF16)
    qp = qp_ref[0].astype(BF16)

    def wait_and_score(c):
        for cp in chunk_copies(b, c):
            cp.wait()
        slot = c % n_slots
        kk = k_buf[slot].astype(BF16)
        kpt = jnp.concatenate([kpt_buf[slot, k] for k in range(n_pg)], axis=1).astype(BF16)
        return _dot_nt(q, kk) + _dot(qp, kpt), kk

    scored_next = wait_and_score(0)
    for c in range(n_chunks):
        nxt = c + ahead
        if nxt < n_chunks:
            start_chunk(b, nxt)
        else:
            @pl.when(b + 1 < pl.num_programs(0))
            def _():
                start_chunk(b + 1, nxt - n_chunks)
        s, kk = scored_next
        if c + 1 < n_chunks:
            scored_next = wait_and_score(c + 1)
        _softmax_step(s, kk, m_scr, l_scr, acc_scr)

    pad = page - t_new
    kn = jnp.concatenate([ckvn_ref[...], jnp.zeros((pad, lat), F32)], axis=0).astype(BF16)
    kpn = jnp.concatenate([kpen_ref[...], jnp.zeros((pad, kpen_ref.shape[1]), F32)], axis=0).astype(BF16)
    s = _dot_nt(q, kn) + _dot_nt(qp, kpn)
    t_q = lax.broadcasted_iota(jnp.int32, (rows, page), 0) & (t_new - 1)
    t_k = lax.broadcasted_iota(jnp.int32, (rows, page), 1)
    _softmax_step(jnp.where(t_k <= t_q, s, NEG_INF), kn, m_scr, l_scr, acc_scr)

    ol = (acc_scr[...] * _softmax_inv_denominator(l_scr)).astype(BF16)
    o_all = _dot(ol, wuv_ref[...])
    vd = wuv_ref.shape[1] // N_HEADS
    for hd in range(N_HEADS):
        o = o_all[hd * t_new:(hd + 1) * t_new, hd * vd:(hd + 1) * vd]
        y_ref[:, hd * vd:(hd + 1) * vd] = o * _silu(z_ref[:, hd * vd:(hd + 1) * vd])


def _attn_sample(qlat, qpe, cache_ckv, cache_kpe_t, page_table, ckv_new, kpe_new, z, w_uv_all):
    nseq, rows, lat = qlat.shape
    rope = qpe.shape[2]
    page = cache_ckv.shape[1]
    n_pages = page_table.shape[1]
    t_new = ckv_new.shape[0] // nseq
    d_attn = z.shape[1]
    n_pg = 16
    n_chunks = n_pages // n_pg
    n_slots = 4

    grid_spec = pltpu.PrefetchScalarGridSpec(
        num_scalar_prefetch=1,
        grid=(nseq,),
        in_specs=[
            pl.BlockSpec((1, rows, lat), lambda b, pt: (b, 0, 0)),
            pl.BlockSpec((1, rows, rope), lambda b, pt: (b, 0, 0)),
            pl.BlockSpec((t_new, lat), lambda b, pt: (b, 0)),
            pl.BlockSpec((t_new, rope), lambda b, pt: (b, 0)),
            pl.BlockSpec((t_new, d_attn), lambda b, pt: (b, 0)),
            _resident(w_uv_all.shape, lambda b, pt: (0, 0)),
            pl.BlockSpec(memory_space=pl.ANY),
            pl.BlockSpec(memory_space=pl.ANY),
        ],
        out_specs=pl.BlockSpec((t_new, d_attn), lambda b, pt: (b, 0)),
        scratch_shapes=[pltpu.VMEM((rows, LANES), F32), pltpu.VMEM((rows, LANES), F32),
                        pltpu.VMEM((rows, lat), F32),
                        pltpu.VMEM((n_slots, n_pg * page, lat), F32),
                        pltpu.VMEM((n_slots, n_pg, rope, page), F32),
                        pltpu.SemaphoreType.DMA((2, n_slots))],
    )
    return pl.pallas_call(
        functools.partial(_attn_sample_body, n_pg=n_pg, n_chunks=n_chunks),
        grid_spec=grid_spec,
        out_shape=jax.ShapeDtypeStruct((nseq * t_new, d_attn), F32),
        compiler_params=_params(("arbitrary",), 48),
        name="attn_sample",
    )(page_table, qlat, qpe, ckv_new, kpe_new, z, w_uv_all, cache_ckv, cache_kpe_t)


def _rope_tables(pos):
    half = QK_ROPE_DIM // 2
    lane = jnp.arange(2 * QK_ROPE_DIM)
    inv_freq = ROPE_THETA ** (-(lane % half).astype(F32) / half)
    ang = pos.astype(F32)[:, None] * inv_freq[None, :]
    sign = jnp.where((lane // half) % 2 == 0, -1.0, 1.0).astype(F32)
    return jnp.cos(ang), jnp.sin(ang) * sign[None, :]


def _rope_tables_transposed(pos):
    half = QK_ROPE_DIM // 2
    inv_freq = ROPE_THETA ** (-jnp.arange(half, dtype=F32) / half)
    ang = inv_freq[:, None] * pos.astype(F32)[None, :]
    return jnp.cos(ang), jnp.sin(ang)


def _swap_halves(w, half):
    return jnp.concatenate([w[..., half:], w[..., :half]], axis=-1)


def kernel(x_prompt, x_sample, state_conv, cache_ckv, cache_kpe, page_table, c_prompt, c_sample, w_ada, b_ada, g_pre, g_post, conv_w_in, conv_w, conv_w_out, kv_g_in, kv_w_down, kv_g_latent, kv_w_up, mla_w_in, mla_g_q, mla_w_uq, mla_w_out):
    nb, seq, d = x_prompt.shape
    nd, t_dec, _ = x_sample.shape
    depth = w_ada.shape[0]
    n_a = conv_w_in.shape[0]
    assert depth == 2 and n_a == 1 and mla_w_in.shape[0] == 1
    lat = kv_g_latent.shape[0]
    rope = QK_ROPE_DIM
    half = rope // 2
    past_len = page_table.shape[1] * cache_ckv.shape[1]

    n_c = nb + nd
    n_c_pad = -(-n_c // 8) * 8
    c_all = jnp.concatenate([c_prompt, c_sample, jnp.zeros((n_c_pad - n_c, d), F32)], axis=0)
    mod = _ada(c_all, w_ada, b_ada)
    mod_p = [mod[l, :nb].reshape(nb, 1, 3 * d) for l in range(depth)]
    mod_s = [mod[l, nb:n_c].reshape(nd, 1, 3 * d) for l in range(depth)]

    w_in0 = conv_w_in[0].astype(BF16)
    w_out0 = conv_w_out[0].astype(BF16)
    y_p, conv_p = _conv_in(x_prompt, mod_p[0], g_pre[0], None, w_in0, conv_w[0], sample=False)
    xp = _out_proj(y_p, x_prompt, mod_p[0], g_post[0], w_out0, name="out_proj_conv_prompt")
    y_s, conv_s = _conv_in(x_sample, mod_s[0], g_pre[0], state_conv[0], w_in0, conv_w[0], sample=True)
    xs = _out_proj(y_s, x_sample, mod_s[0], g_post[0], w_out0, name="out_proj_conv_sample")

    w_kpe = kv_w_down[:, lat:]
    w_kpe_sw = _swap_halves(w_kpe, half)
    w_down_ext = jnp.concatenate([kv_w_down[:, :lat], w_kpe, w_kpe, w_kpe_sw, w_kpe_sw], axis=1).astype(BF16)
    qr = mla_g_q.shape[1]
    w_uq = mla_w_uq[0].reshape(qr, N_HEADS, QK_NOPE_DIM + rope)
    w_uq_pe = w_uq[:, :, QK_NOPE_DIM:]
    w_uq_ext = jnp.concatenate(
        [w_uq[:, :, :QK_NOPE_DIM].reshape(qr, -1), w_uq_pe.reshape(qr, -1),
         _swap_halves(w_uq_pe, half).reshape(qr, -1)], axis=1).astype(BF16)
    w_uk_t = jnp.transpose(kv_w_up[:, :, :QK_NOPE_DIM], (1, 2, 0)).astype(BF16)
    w_uk_all = kv_w_up[:, :, :QK_NOPE_DIM].reshape(lat, -1).astype(BF16)
    w_uv_all = kv_w_up[:, :, QK_NOPE_DIM:].reshape(lat, -1).astype(BF16)
    w_uv_t_all = jnp.transpose(kv_w_up[:, :, QK_NOPE_DIM:], (1, 2, 0)).reshape(-1, lat).astype(BF16)
    w_uq_t = mla_w_uq[0].T.astype(BF16)
    w_in1 = mla_w_in[0].astype(BF16)
    w_out1 = mla_w_out[0].astype(BF16)

    cos_p, sin_p = _rope_tables(jnp.arange(seq))
    cos_s, sin_s = _rope_tables(past_len + jnp.arange(t_dec))
    reps = MLA_TILE_ROWS // t_dec
    cos_s, sin_s = jnp.tile(cos_s, (reps, 1)), jnp.tile(sin_s, (reps, 1))
    cos_t, sin_t = _rope_tables_transposed(jnp.arange(seq))

    ckv_p, kpe_p, z_p, k_heads, v_t, q_t = _mla_in(
        xp, mod_p[1], g_pre[1], kv_g_in, w_down_ext, kv_g_latent, cos_p, sin_p, w_in1, mla_g_q[0],
        (cos_t, sin_t, w_uq_t, w_uk_all, w_uv_t_all), sample=False)
    y_p = _attn_prompt(q_t, k_heads, v_t, z_p, nseq=nb)
    xp = _out_proj(y_p, xp, mod_p[1], g_post[1], w_out1, name="out_proj_mla_prompt")

    ckv_s, kpe_s, z_s, qlat_s, qpe_s = _mla_in(
        xs, mod_s[1], g_pre[1], kv_g_in, w_down_ext, kv_g_latent, cos_s, sin_s, w_in1, mla_g_q[0],
        (w_uq_ext, w_uk_t), sample=True)
    cache_kpe_t = jnp.swapaxes(cache_kpe, 1, 2)
    y_s = _attn_sample(qlat_s, qpe_s, cache_ckv, cache_kpe_t, page_table, ckv_s, kpe_s, z_s, w_uv_all)
    xs = _out_proj(y_s, xs, mod_s[1], g_post[1], w_out1, name="out_proj_mla_sample")

    return (xp, xs, conv_p[None], conv_s[None],
            ckv_p.reshape(nb, seq, lat), kpe_p.reshape(nb, seq, rope),
            ckv_s.reshape(nd, t_dec, lat), kpe_s.reshape(nd, t_dec, rope))
```
